```python
import jax, jax.numpy as jnp
from jax import lax
import numpy as np

D_MODEL = 1024
BATCH = 4
SEQ = 4096
DEPTH = 1
DEC_BATCH = 32
DEC_SEQ = 2048
PAST_LEN = 128

N_META = 16
MLA_HEADS = 8
QK_NOPE = 64
QK_ROPE = 32
V_HEAD = 64
Q_LORA = 384
KV_LORA = 256
ROPE_THETA = 10000.0
Q_BLOCK = 128
GDN_HEADS = 8
GDN_DK = 64
GDN_DV = 64
CONV_K = 4
CHUNK = 64
MLA_W = MLA_HEADS * V_HEAD
GDN_W = GDN_HEADS * GDN_DV
MIX_W = MLA_W + GDN_W
QKV_W = GDN_HEADS * (2 * GDN_DK + GDN_DV)
D_FF = 4 * D_MODEL
IN_SPLITS = (Q_LORA, KV_LORA, QK_ROPE, QKV_W, GDN_W, GDN_HEADS, GDN_HEADS, GDN_HEADS, GDN_HEADS)
N_IN = Q_LORA + KV_LORA + QK_ROPE + QKV_W + GDN_W + 4 * GDN_HEADS
DN_ALPHA = (2 * DEPTH) ** 0.25
DN_BETA = (8 * DEPTH) ** -0.25
LN_EPS = 1e-5
RMS_EPS = 1e-6

kernel_name = 'hymba_mla_bigdn_deepnorm_encoder'


def _layer_norm(x, g, b):
    xf = x.astype(jnp.float32)
    mu = jnp.mean(xf, -1, keepdims=True)
    var = jnp.mean(jnp.square(xf - mu), -1, keepdims=True)
    return ((xf - mu) * lax.rsqrt(var + LN_EPS) * g.astype(jnp.float32) + b.astype(jnp.float32)).astype(x.dtype)


def _rms_norm(x, g):
    xf = x.astype(jnp.float32)
    return (xf * lax.rsqrt(jnp.mean(xf * xf, -1, keepdims=True) + RMS_EPS) * g.astype(jnp.float32)).astype(x.dtype)


def _l2norm(x):
    return x * lax.rsqrt(jnp.sum(x * x, -1, keepdims=True) + 1e-6)


def _rope_tables(L, dtype):
    inv = ROPE_THETA ** (-jnp.arange(0, QK_ROPE, 2, dtype=jnp.float32) / QK_ROPE)
    ang = jnp.arange(L, dtype=jnp.float32)[:, None] * inv[None, :]
    return jnp.cos(ang)[:, None, :].astype(dtype), jnp.sin(ang)[:, None, :].astype(dtype)


def _rope(x, cos, sin):
    x1, x2 = jnp.split(x, 2, axis=-1)
    return jnp.concatenate([x1 * cos - x2 * sin, x2 * cos + x1 * sin], axis=-1)


def _block_attention(q, k, v):
    B, L, H, Dq = q.shape
    nblk = -(-L // Q_BLOCK)
    qp = jnp.pad(q, ((0, 0), (0, nblk * Q_BLOCK - L), (0, 0), (0, 0)))
    qb = jnp.moveaxis(qp.reshape(B, nblk, Q_BLOCK, H, Dq), 1, 0)
    scale = Dq ** -0.5

    def one(qblk):
        s = jnp.einsum('bqhd,bkhd->bhqk', qblk, k, preferred_element_type=jnp.float32) * scale
        p = jax.nn.softmax(s, axis=-1).astype(v.dtype)
        return jnp.einsum('bhqk,bkhd->bqhd', p, v)

    o = lax.map(one, qb)
    return jnp.moveaxis(o, 0, 1).reshape(B, nblk * Q_BLOCK, H, v.shape[-1])[:, :L]


def _centred_conv(x, w):
    left = (CONV_K - 1) // 2
    return lax.conv_general_dilated(x, w[:, None, :].astype(x.dtype), window_strides=(1,),
                                    padding=[(left, CONV_K - 1 - left)],
                                    dimension_numbers=('NWC', 'WIO', 'NWC'),
                                    feature_group_count=x.shape[-1])


def _gated_delta_chunked(q, k, v, g, beta):
    Bt, Lp, H, DK = q.shape
    DV = v.shape[-1]
    N = Lp // CHUNK

    def blk(t):
        return jnp.moveaxis(t.reshape((Bt, N, CHUNK, H) + t.shape[3:]), 3, 1)

    q, k, v, g, beta = blk(q) * (DK ** -0.5), blk(k), blk(v), blk(g), blk(beta)
    gc = jnp.cumsum(g, axis=-1)
    tri = jnp.tril(jnp.ones((CHUNK, CHUNK), bool))
    strict = jnp.tril(jnp.ones((CHUNK, CHUNK), bool), -1)
    decay = jnp.exp(jnp.where(tri, gc[..., :, None] - gc[..., None, :], -jnp.inf))
    kb = k * beta[..., None]
    m = jnp.where(strict, jnp.einsum('bhncd,bhnsd->bhncs', kb, k) * decay, 0.0)
    rhs = jnp.concatenate([v * beta[..., None], kb * jnp.exp(gc)[..., None]], axis=-1)
    sol = lax.linalg.triangular_solve(m + jnp.eye(CHUNK, dtype=jnp.float32), rhs, left_side=True,
                                      lower=True, unit_diagonal=True)
    u, w = sol[..., :DV], sol[..., DV:]
    attn = jnp.einsum('bhncd,bhnsd->bhncs', q, k) * decay
    qg = q * jnp.exp(gc)[..., None]
    glast = gc[..., -1]
    kdec = k * jnp.exp(glast[..., None] - gc)[..., None]

    def step(S, xs):
        qg_i, w_i, u_i, kd_i, at_i, gl_i = xs
        v_new = u_i - jnp.einsum('bhcd,bhde->bhce', w_i, S)
        o_i = jnp.einsum('bhcd,bhde->bhce', qg_i, S) + jnp.einsum('bhcs,bhse->bhce', at_i, v_new)
        S = S * jnp.exp(gl_i)[..., None, None] + jnp.einsum('bhcd,bhce->bhde', kd_i, v_new)
        return S, o_i

    xs = tuple(jnp.moveaxis(t, 2, 0) for t in (qg, w, u, kdec, attn, glast))
    S0 = jnp.zeros((Bt, H, DK, DV), jnp.float32)
    _, o = lax.scan(step, S0, xs)
    return jnp.transpose(o, (1, 0, 3, 2, 4)).reshape(Bt, Lp, H, DV)


def _gdn_branch(qkv, z, a_f, a_b, b_f, b_b, conv_w, a_log_f, a_log_b, dt_bias_f, dt_bias_b, gdn_norm_g):
    B, L, _ = qkv.shape
    f32 = jnp.float32
    qkv = jax.nn.silu(_centred_conv(qkv, conv_w)).astype(f32)
    q, k, v = jnp.split(qkv, [GDN_HEADS * GDN_DK, 2 * GDN_HEADS * GDN_DK], axis=-1)
    q = _l2norm(q.reshape(B, L, GDN_HEADS, GDN_DK))
    k = _l2norm(k.reshape(B, L, GDN_HEADS, GDN_DK))
    v = v.reshape(B, L, GDN_HEADS, GDN_DV)

    def gate(a, b, a_log, dt_bias):
        gl = -jnp.exp(a_log.astype(f32)) * jax.nn.softplus(a.astype(f32) + dt_bias.astype(f32))
        return gl, jax.nn.sigmoid(b.astype(f32))

    g_f, beta_f = gate(a_f, b_f, a_log_f, dt_bias_f)
    g_b, beta_b = gate(a_b, b_b, a_log_b, dt_bias_b)
    pad = (-N_META) % CHUNK

    def both(tf, tb):
        widths = ((0, 0), (pad, 0)) + ((0, 0),) * (tf.ndim - 2)
        return jnp.concatenate([jnp.pad(tf, widths), jnp.flip(jnp.pad(tb, widths), axis=1)], axis=0)

    o = _gated_delta_chunked(both(q, q), both(k, k), both(v, v), both(g_f, g_b), both(beta_f, beta_b))
    o = (o[:B] + jnp.flip(o[B:], axis=1))[:, pad:]
    o = o * lax.rsqrt(jnp.mean(o * o, -1, keepdims=True) + RMS_EPS) * gdn_norm_g.astype(f32)
    o = o * jax.nn.silu(z.astype(f32).reshape(B, L, GDN_HEADS, GDN_DV))
    return o.reshape(B, L, GDN_W)


def _mixer(h, cos, sin, w_in, g_cq, g_ckv, w_uq, w_uk, w_uv, conv_w, a_log_f, a_log_b,
           dt_bias_f, dt_bias_b, gdn_norm_g, w_out):
    B, L, _ = h.shape
    proj = h @ w_in
    idx = np.cumsum(IN_SPLITS)[:-1].tolist()
    c_q, c_kv, k_r, qkv, z, a_f, a_b, b_f, b_b = jnp.split(proj, idx, axis=-1)
    q = (_rms_norm(c_q, g_cq) @ w_uq).reshape(B, L, MLA_HEADS, QK_NOPE + QK_ROPE)
    q = jnp.concatenate([q[..., :QK_NOPE], _rope(q[..., QK_NOPE:], cos, sin)], axis=-1)
    c_kv = _rms_norm(c_kv, g_ckv)
    k_nope = (c_kv @ w_uk).reshape(B, L, MLA_HEADS, QK_NOPE)
    v = (c_kv @ w_uv).reshape(B, L, MLA_HEADS, V_HEAD)
    k_r = jnp.broadcast_to(_rope(k_r[:, :, None, :], cos, sin), (B, L, MLA_HEADS, QK_ROPE))
    k = jnp.concatenate([k_nope, k_r], axis=-1)
    mla_out = _block_attention(q, k, v).reshape(B, L, MLA_W)
    gdn_out = _gdn_branch(qkv, z, a_f, a_b, b_f, b_b, conv_w, a_log_f, a_log_b, dt_bias_f, dt_bias_b,
                          gdn_norm_g).astype(h.dtype)
    return jnp.concatenate([mla_out, gdn_out], axis=-1) @ w_out


def _encode(x, meta_tokens, ln_in_g, ln_in_b, w_in, g_cq, g_ckv, w_uq, w_uk, w_uv, conv_w, a_log_f, a_log_b,
            dt_bias_f, dt_bias_b, gdn_norm_g, w_out, ln1_g, ln1_b, w_ff1, w_ff2, ln2_g, ln2_b):
    B = x.shape[0]
    h = jnp.concatenate([jnp.broadcast_to(meta_tokens.astype(x.dtype)[None], (B, N_META, D_MODEL)), x], axis=1)
    h = _layer_norm(h, ln_in_g, ln_in_b)
    cos, sin = _rope_tables(h.shape[1], h.dtype)
    for l in range(DEPTH):
        mix = _mixer(h, cos, sin, w_in[l], g_cq[l], g_ckv[l], w_uq[l], w_uk[l], w_uv[l], conv_w[l],
                     a_log_f[l], a_log_b[l], dt_bias_f[l], dt_bias_b[l], gdn_norm_g[l], w_out[l])
        h = _layer_norm(DN_ALPHA * h + mix, ln1_g[l], ln1_b[l])
        ff = jnp.square(jax.nn.relu(h @ w_ff1[l])) @ w_ff2[l]
        h = _layer_norm(DN_ALPHA * h + ff, ln2_g[l], ln2_b[l])
    return h[:, N_META:]


def setup_inputs(seed: int = 0) -> dict:
    key = jax.random.key(seed)
    ks = jax.random.split(key, 32)
    f32 = jnp.float32

    def nrm(i, shape, scale):
        return jax.random.normal(ks[i], shape, f32) * scale

    def gain(i, shape):
        return 1.0 + 0.02 * jax.random.normal(ks[i], shape, f32)

    def dt_bias(i):
        dt = jnp.exp(jax.random.uniform(ks[i], (DEPTH, GDN_HEADS), f32, np.log(1e-3), np.log(1e-1)))
        return dt + jnp.log(-jnp.expm1(-dt))

    return {
        'x_prompt': nrm(0, (BATCH, SEQ, D_MODEL), 1.0),
        'x_sample': nrm(1, (DEC_BATCH, DEC_SEQ, D_MODEL), 1.0),
        'meta_tokens': nrm(2, (N_META, D_MODEL), 1.0),
        'ln_in_g': gain(3, (D_MODEL,)),
        'ln_in_b': nrm(4, (D_MODEL,), 0.02),
        'w_in': nrm(5, (DEPTH, D_MODEL, N_IN), D_MODEL ** -0.5),
        'g_cq': gain(6, (DEPTH, Q_LORA)),
        'g_ckv': gain(7, (DEPTH, KV_LORA)),
        'w_uq': nrm(8, (DEPTH, Q_LORA, MLA_HEADS * (QK_NOPE + QK_ROPE)), Q_LORA ** -0.5),
        'w_uk': nrm(9, (DEPTH, KV_LORA, MLA_HEADS * QK_NOPE), KV_LORA ** -0.5),
        'w_uv': nrm(10, (DEPTH, KV_LORA, MLA_HEADS * V_HEAD), KV_LORA ** -0.5),
        'conv_w': nrm(11, (DEPTH, CONV_K, QKV_W), CONV_K ** -0.5),
        'a_log_f': jnp.log(jax.random.uniform(ks[12], (DEPTH, GDN_HEADS), f32, 1.0, 16.0)),
        'a_log_b': jnp.log(jax.random.uniform(ks[13], (DEPTH, GDN_HEADS), f32, 1.0, 16.0)),
        'dt_bias_f': dt_bias(14),
        'dt_bias_b': dt_bias(15),
        'gdn_norm_g': gain(16, (DEPTH, GDN_DV)),
        'w_out': nrm(17, (DEPTH, MIX_W, D_MODEL), MIX_W ** -0.5 * DN_BETA),
        'ln1_g': gain(18, (DEPTH, D_MODEL)),
        'ln1_b': nrm(19, (DEPTH, D_MODEL), 0.02),
        'w_ff1': nrm(20, (DEPTH, D_MODEL, D_FF), D_MODEL ** -0.5),
        'w_ff2': nrm(21, (DEPTH, D_FF, D_MODEL), D_FF ** -0.5 * DN_BETA),
        'ln2_g': gain(22, (DEPTH, D_MODEL)),
        'ln2_b': nrm(23, (DEPTH, D_MODEL), 0.02),
    }


def reference(x_prompt, x_sample, meta_tokens, ln_in_g, ln_in_b, w_in, g_cq, g_ckv, w_uq, w_uk, w_uv, conv_w,
              a_log_f, a_log_b, dt_bias_f, dt_bias_b, gdn_norm_g, w_out, ln1_g, ln1_b, w_ff1, w_ff2, ln2_g, ln2_b):
    y_prompt = _encode(x_prompt, meta_tokens, ln_in_g, ln_in_b, w_in, g_cq, g_ckv, w_uq, w_uk, w_uv, conv_w,
                       a_log_f, a_log_b, dt_bias_f, dt_bias_b, gdn_norm_g, w_out, ln1_g, ln1_b, w_ff1, w_ff2,
                       ln2_g, ln2_b)
    y_sample = _encode(x_sample, meta_tokens, ln_in_g, ln_in_b, w_in, g_cq, g_ckv, w_uq, w_uk, w_uv, conv_w,
                       a_log_f, a_log_b, dt_bias_f, dt_bias_b, gdn_norm_g, w_out, ln1_g, ln1_b, w_ff1, w_ff2,
                       ln2_g, ln2_b)
    return (y_prompt, y_sample)
```

```python
import functools

import jax
import jax.numpy as jnp
from jax import lax
from jax.experimental import pallas as pl
from jax.experimental.pallas import tpu as pltpu

F32 = jnp.float32
BF16 = jnp.bfloat16

D_MODEL = 1024
N_META = 16
HEADS = 8
QK_NOPE = 64
QK_ROPE = 32
V_HEAD = 64
Q_LORA = 384
KV_LORA = 256
ROPE_THETA = 10000.0
GDN_DK = 64
GDN_DV = 64
CONV_K = 4
GDN_W = HEADS * GDN_DV
QKV_W = 3 * GDN_W
D_FF = 4 * D_MODEL
DEPTH = 1
DN_ALPHA = (2 * DEPTH) ** 0.25
LN_EPS = 1e-5
RMS_EPS = 1e-6
L2_EPS = 1e-6

LANE = 128
CHUNK = 128
N_PAIRS = HEADS // 2
COL_CQ = 0
COL_CKV = COL_CQ + Q_LORA
COL_SMALL = COL_CKV + KV_LORA
COL_QKV = COL_SMALL + LANE
COL_Z = COL_QKV + QKV_W
N_IN_PACKED = COL_Z + GDN_W
VMEM_LIMIT = 56 * 1024 * 1024
ROW_TILE = 512
GDN_BLOCK = 512
Q_TILE = 256


def _cparams(sem):
    return pltpu.CompilerParams(dimension_semantics=sem, vmem_limit_bytes=VMEM_LIMIT)


def _layer_norm(x, g, b):
    mu = jnp.mean(x, axis=-1, keepdims=True)
    xc = x - mu
    var = jnp.mean(xc * xc, axis=-1, keepdims=True)
    return xc * lax.rsqrt(var + LN_EPS) * g + b


def _rms_norm(x, g):
    return x * lax.rsqrt(jnp.mean(x * x, axis=-1, keepdims=True) + RMS_EPS) * g


def _sigmoid(x):
    return 1.0 / (1.0 + jnp.exp(-x))


def _softplus(x):
    return jnp.maximum(x, 0.0) + jnp.log1p(jnp.exp(-jnp.abs(x)))


def _dot(a, b):
    return jnp.dot(a, b, preferred_element_type=F32)


def _inproj_kernel(x_ref, lng_ref, lnb_ref, win_ref, gcq_ref, gckv_ref, wuq_ref, wkv_ref, tab_ref,
                   alog_ref, dtb_ref, q_ref, k_ref, v_ref, qkv_ref, z_ref, gate_ref, *, scan_len):
    tm = x_ref.shape[0]
    h = _layer_norm(x_ref[...], lng_ref[...], lnb_ref[...])
    proj = _dot(h.astype(BF16), win_ref[...])

    qkv_ref[...] = proj[:, COL_QKV:COL_QKV + QKV_W]
    z_ref[...] = proj[:, COL_Z:COL_Z + GDN_W]

    cqn = _rms_norm(proj[:, COL_CQ:COL_CQ + Q_LORA], gcq_ref[...])
    qfull = _dot(cqn.astype(BF16), wuq_ref[...])
    cq_t, s1q_t, s2q_t = tab_ref[0], tab_ref[1], tab_ref[2]
    for hd in range(HEADS):
        qh = qfull[:, hd * LANE:(hd + 1) * LANE]
        qr = qh * cq_t + pltpu.roll(qh, LANE - 16, 1) * s1q_t + pltpu.roll(qh, 16, 1) * s2q_t
        q_ref[:, hd * LANE:(hd + 1) * LANE] = qr.astype(BF16)

    small = proj[:, COL_SMALL:COL_SMALL + LANE]
    ck_t, s1k_t, s2k_t = tab_ref[3], tab_ref[4], tab_ref[5]
    krope = small * ck_t + pltpu.roll(small, LANE - 16, 1) * s1k_t + pltpu.roll(small, 16, 1) * s2k_t
    ckvn = _rms_norm(proj[:, COL_CKV:COL_CKV + KV_LORA], gckv_ref[...])
    kv = _dot(ckvn.astype(BF16), wkv_ref[...])
    for hd in range(HEADS):
        k_ref[:, hd * LANE:(hd + 1) * LANE] = (kv[:, hd * LANE:(hd + 1) * LANE] + krope).astype(BF16)
    v_ref[...] = kv[:, HEADS * LANE:].astype(BF16)

    lane = lax.broadcasted_iota(jnp.int32, (tm, LANE), 1)
    row = lax.broadcasted_iota(jnp.int32, (tm, LANE), 0)
    gval = jnp.where(lane < 16, -jnp.exp(alog_ref[...]) * _softplus(small + dtb_ref[...]), _sigmoid(small))
    pos = row % scan_len
    pre = gval
    suf = gval
    sh = 1
    while sh < scan_len:
        pre = pre + jnp.where(pos >= sh, pltpu.roll(pre, sh, 0), 0.0)
        suf = suf + jnp.where(pos < scan_len - sh, pltpu.roll(suf, tm - sh, 0), 0.0)
        sh *= 2
    gate_ref[:, 0:LANE] = jnp.where(lane < 8, pre, gval)
    gate_ref[:, LANE:2 * LANE] = pltpu.roll(jnp.where((lane >= 8) & (lane < 16), suf, gval), LANE - 8, 1)


def _inproj(x, tabs, prm, *, tm, seq_blocks):
    rows = x.shape[0]
    n = rows // tm
    const = lambda i: (0, 0)
    kernel = functools.partial(_inproj_kernel, scan_len=min(tm, CHUNK))
    return pl.pallas_call(
        kernel,
        out_shape=(
            jax.ShapeDtypeStruct((rows, HEADS * LANE), BF16),
            jax.ShapeDtypeStruct((rows, HEADS * LANE), BF16),
            jax.ShapeDtypeStruct((rows, HEADS * V_HEAD), BF16),
            jax.ShapeDtypeStruct((rows, QKV_W), F32),
            jax.ShapeDtypeStruct((rows, GDN_W), F32),
            jax.ShapeDtypeStruct((rows, 2 * LANE), F32),
        ),
        grid=(n,),
        in_specs=[
            pl.BlockSpec((tm, D_MODEL), lambda i: (i, 0)),
            pl.BlockSpec((1, D_MODEL), const),
            pl.BlockSpec((1, D_MODEL), const),
            pl.BlockSpec((D_MODEL, N_IN_PACKED), const),
            pl.BlockSpec((1, Q_LORA), const),
            pl.BlockSpec((1, KV_LORA), const),
            pl.BlockSpec((Q_LORA, HEADS * LANE), const),
            pl.BlockSpec((KV_LORA, HEADS * LANE + HEADS * V_HEAD), const),
            pl.BlockSpec((6, tm, LANE), lambda i: (0, i % seq_blocks, 0)),
            pl.BlockSpec((1, LANE), const),
            pl.BlockSpec((1, LANE), const),
        ],
        out_specs=(
            pl.BlockSpec((tm, HEADS * LANE), lambda i: (i, 0)),
            pl.BlockSpec((tm, HEADS * LANE), lambda i: (i, 0)),
            pl.BlockSpec((tm, HEADS * V_HEAD), lambda i: (i, 0)),
            pl.BlockSpec((tm, QKV_W), lambda i: (i, 0)),
            pl.BlockSpec((tm, GDN_W), lambda i: (i, 0)),
            pl.BlockSpec((tm, 2 * LANE), lambda i: (i, 0)),
        ),
        compiler_params=_cparams(("arbitrary",)),
        name="inproj",
    )(x, prm["ln_in_g"], prm["ln_in_b"], prm["w_in"], prm["g_cq"], prm["g_ckv"], prm["w_uq"], prm["w_kv"],
      tabs, prm["alog"], prm["dtb"])


def _pair_rsqrt_norm(x, eps, scale):
    lane = lax.broadcasted_iota(jnp.int32, x.shape, 1)
    left = lane < 64
    sq = x * x
    sa = jnp.sum(jnp.where(left, sq, 0.0), axis=-1, keepdims=True)
    sb = jnp.sum(jnp.where(left, 0.0, sq), axis=-1, keepdims=True)
    return jnp.where(left, lax.rsqrt(sa * scale + eps), lax.rsqrt(sb * scale + eps))


def _conv_kernel(x_ref, prev_ref, next_ref, pfirst_ref, nlast_ref, w_ref, q_ref, k_ref, v_ref):
    i = pl.program_id(1)
    nblk = pl.num_programs(1)
    blk = x_ref.shape[0]
    x = x_ref[...]
    prev = jnp.where(i == 0, pfirst_ref[...], prev_ref[...])
    nxt = jnp.where(i == nblk - 1, nlast_ref[...], next_ref[...])
    row = lax.broadcasted_iota(jnp.int32, x.shape, 0)
    xm1 = jnp.where(row == 0, prev[7:8, :], pltpu.roll(x, 1, 0))
    xp1 = jnp.where(row == blk - 1, nxt[0:1, :], pltpu.roll(x, blk - 1, 0))
    xp2 = jnp.where(row == blk - 2, nxt[0:1, :], jnp.where(row == blk - 1, nxt[1:2, :], pltpu.roll(x, blk - 2, 0)))
    w = w_ref[...]
    y = xm1 * w[0:1, :] + x * w[1:2, :] + xp1 * w[2:3, :] + xp2 * w[3:4, :]
    y = y * _sigmoid(y)
    for p in range(N_PAIRS):
        qp = y[:, p * LANE:(p + 1) * LANE]
        kp = y[:, GDN_W + p * LANE:GDN_W + (p + 1) * LANE]
        q_ref[:, p * LANE:(p + 1) * LANE] = qp * _pair_rsqrt_norm(qp, L2_EPS, 1.0) * (GDN_DK ** -0.5)
        k_ref[:, p * LANE:(p + 1) * LANE] = kp * _pair_rsqrt_norm(kp, L2_EPS, 1.0)
    v_ref[...] = y[:, 2 * GDN_W:]


def _conv(x, prev_arr, next_arr, pfirst, nlast, conv_w, *, batch, blk, nblk, main_map, prev_map, next_map,
          pfirst_map, nlast_map):
    rows = batch * nblk * blk
    return pl.pallas_call(
        _conv_kernel,
        out_shape=tuple(jax.ShapeDtypeStruct((rows, GDN_W), F32) for _ in range(3)),
        grid=(batch, nblk),
        in_specs=[
            pl.BlockSpec((blk, QKV_W), main_map),
            pl.BlockSpec((8, QKV_W), prev_map),
            pl.BlockSpec((8, QKV_W), next_map),
            pl.BlockSpec((8, QKV_W), pfirst_map),
            pl.BlockSpec((8, QKV_W), nlast_map),
            pl.BlockSpec((CONV_K, QKV_W), lambda b, i: (0, 0)),
        ],
        out_specs=tuple(pl.BlockSpec((blk, GDN_W), lambda b, i: (b * nblk + i, 0)) for _ in range(3)),
        compiler_params=_cparams(("arbitrary", "arbitrary")),
        name="conv_silu_l2",
    )(x, prev_arr, next_arr, pfirst, nlast, conv_w)


def _attn_kernel(q_ref, k_ref, v_ref, km_ref, vm_ref, o_ref):
    dn = (((1,), (1,)), ((), ()))
    v = v_ref[...]
    vm = vm_ref[...]
    outs = []
    for hh in range(2):
        q = q_ref[:, hh * LANE:(hh + 1) * LANE]
        s = lax.dot_general(q, k_ref[:, hh * LANE:(hh + 1) * LANE], dn, preferred_element_type=F32)
        sm = lax.dot_general(q, km_ref[:, hh * LANE:(hh + 1) * LANE], dn, preferred_element_type=F32)
        m = jnp.maximum(jnp.max(s, axis=-1, keepdims=True), jnp.max(sm, axis=-1, keepdims=True))
        p = jnp.exp(s - m)
        pm = jnp.exp(sm - m)
        den = jnp.sum(p, axis=-1, keepdims=True) + jnp.sum(pm, axis=-1, keepdims=True)
        ov = _dot(p.astype(BF16), v) + _dot(pm.astype(BF16), vm)
        outs.append(ov / den)
    lane = lax.broadcasted_iota(jnp.int32, outs[0].shape, 1)
    o_ref[...] = jnp.where(lane < 64, outs[0], outs[1]).astype(BF16)


def _attention(q, k, v, km, vm, *, batch, seq, tq):
    nq = seq // tq
    rows = batch * seq
    return pl.pallas_call(
        _attn_kernel,
        out_shape=jax.ShapeDtypeStruct((rows, HEADS * V_HEAD), BF16),
        grid=(batch, N_PAIRS, nq),
        in_specs=[
            pl.BlockSpec((tq, 2 * LANE), lambda b, p, i: (b * nq + i, p)),
            pl.BlockSpec((seq, 2 * LANE), lambda b, p, i: (b, p)),
            pl.BlockSpec((seq, LANE), lambda b, p, i: (b, p)),
            pl.BlockSpec((N_META, 2 * LANE), lambda b, p, i: (0, p)),
            pl.BlockSpec((N_META, LANE), lambda b, p, i: (0, p)),
        ],
        out_specs=pl.BlockSpec((tq, LANE), lambda b, p, i: (b * nq + i, p)),
        compiler_params=_cparams(("arbitrary", "arbitrary", "arbitrary")),
        name="mla_attention",
    )(q, k, v, km, vm)


def _blockdiag_rows(x):
    lane = lax.broadcasted_iota(jnp.int32, x.shape, 1)
    zero = jnp.zeros_like(x)
    return jnp.concatenate([jnp.where(lane < 64, x, zero), jnp.where(lane < 64, zero, x)], axis=0)


def _gdn_kernel(q_ref, k_ref, v_ref, g_ref, z_ref, qm_ref, km_ref, vm_ref, gm_ref, gn_ref, o_ref,
                s_scr, of_scr, *, nblk, nchunk):
    t = pl.program_id(1)
    is_meta = t == 0
    is_bwd = t > nblk
    blk_idx = jnp.where(is_bwd, 2 * nblk - t, jnp.maximum(t - 1, 0))
    blk = nchunk * CHUNK

    @pl.when(jnp.logical_or(t == 0, t == nblk + 1))
    def _():
        s_scr[...] = jnp.zeros_like(s_scr)

    row_i = lax.broadcasted_iota(jnp.int32, (CHUNK, CHUNK), 0)
    col_i = lax.broadcasted_iota(jnp.int32, (CHUNK, CHUNK), 1)
    sgn = jnp.where(is_bwd, -1, 1)
    d_rc = (col_i - row_i) * sgn
    incl = d_rc <= 0
    strict = d_rc < 0
    incl2 = jnp.concatenate([incl, incl], axis=1)
    strict2 = jnp.concatenate([strict, strict], axis=1)
    left = col_i < 64
    top = row_i < 64
    eye = (row_i == col_i).astype(F32)
    eye2 = jnp.concatenate([eye, eye], axis=1)
    bd_mask = top == left
    gn = gn_ref[...]

    def chunk_body(ci, carry):
        c = jnp.where(is_bwd, nchunk - 1 - ci, ci)
        r0 = pl.multiple_of(c * CHUNK, CHUNK)
        rows = pl.ds(r0, CHUNK)
        q_c = jnp.where(is_meta, qm_ref[...], q_ref[rows, :])
        k_c = jnp.where(is_meta, km_ref[...], k_ref[rows, :])
        v_c = jnp.where(is_meta, vm_ref[...], v_ref[rows, :])
        g_c = jnp.where(is_meta, gm_ref[...], g_ref[rows, :])
        g_t = g_c.T
        k_t = k_c.T
        gl_row = jnp.where(is_bwd, g_c[0:1, :], g_c[CHUNK - 1:CHUNK, :])
        gl_col = jnp.where(is_bwd, g_t[:, 0:1], g_t[:, CHUNK - 1:CHUNK])
        egc = jnp.exp(g_c)
        ekd_t = jnp.exp(gl_col - g_t)
        egl = jnp.exp(gl_row)
        of_rows = pl.ds(pl.multiple_of(blk_idx * blk + c * CHUNK, CHUNK), CHUNK)

        for p in range(N_PAIRS):
            ha, hb = 2 * p, 2 * p + 1
            q2 = q_c[:, p * LANE:(p + 1) * LANE]
            k2 = k_c[:, p * LANE:(p + 1) * LANE]
            v2 = v_c[:, p * LANE:(p + 1) * LANE]
            kt2 = k_t[p * LANE:(p + 1) * LANE, :]

            def colb(arr, lane_idx):
                return jnp.broadcast_to(arr[:, lane_idx:lane_idx + 1], (CHUNK, CHUNK))

            def rowb(arr, row_idx):
                return jnp.broadcast_to(arr[row_idx:row_idx + 1, :], (CHUNK, CHUNK))

            gdiff2 = jnp.concatenate([colb(g_c, ha) - rowb(g_t, ha), colb(g_c, hb) - rowb(g_t, hb)], axis=1)
            dec2 = jnp.exp(jnp.where(incl2, gdiff2, -jnp.inf))
            beta_a = colb(g_c, 16 + ha)
            beta_b = colb(g_c, 16 + hb)
            beta2 = jnp.concatenate([beta_a, beta_b], axis=1)
            beta_p = jnp.where(left, beta_a, beta_b)
            egc_p = jnp.where(left, colb(egc, ha), colb(egc, hb))

            zero_t = jnp.zeros_like(kt2)
            w_qk = jnp.concatenate([jnp.where(top, kt2, zero_t), jnp.where(top, zero_t, kt2)], axis=1)
            qkkk = _dot(jnp.concatenate([q2, k2], axis=0).astype(BF16), w_qk.astype(BF16))
            attn2 = qkkk[:CHUNK] * dec2
            m2 = jnp.where(strict2, beta2 * qkkk[CHUNK:] * dec2, 0.0)

            def bd2(x):
                zero = jnp.zeros((CHUNK, CHUNK), x.dtype)
                return jnp.concatenate([jnp.concatenate([x[:, :CHUNK], zero], axis=1),
                                        jnp.concatenate([zero, x[:, CHUNK:]], axis=1)], axis=0)

            npow = -m2
            tinv = eye2 + npow
            npow = _dot(npow.astype(BF16), bd2(npow.astype(BF16)))
            lvl = 2
            while lvl < CHUNK:
                w_n = bd2(npow.astype(BF16))
                if 2 * lvl < CHUNK:
                    both = _dot(jnp.concatenate([tinv, npow], axis=0).astype(BF16), w_n)
                    tinv = tinv + both[:CHUNK]
                    npow = both[CHUNK:]
                else:
                    tinv = tinv + _dot(tinv.astype(BF16), w_n)
                lvl *= 2

            vb2 = v2 * beta_p
            kbe2 = k2 * beta_p * egc_p
            w_rhs = jnp.concatenate([_blockdiag_rows(vb2), _blockdiag_rows(kbe2)], axis=1).astype(BF16)
            uw = _dot(tinv.astype(BF16), w_rhs)
            u2 = uw[:, :LANE]
            w2 = uw[:, LANE:]

            s2 = s_scr[p]
            qg2 = q2 * egc_p
            ws_qs = _dot(jnp.concatenate([w2, qg2], axis=0).astype(BF16), s2.astype(BF16))
            vnew2 = u2 - ws_qs[:CHUNK]
            o2 = ws_qs[CHUNK:] + _dot(attn2.astype(BF16), _blockdiag_rows(vnew2).astype(BF16))
            ekd_rows = jnp.where(top, rowb(ekd_t, ha), rowb(ekd_t, hb))
            kdec_t2 = kt2 * ekd_rows
            egl_p = jnp.where(left[0:1, :], jnp.broadcast_to(egl[:, ha:ha + 1], (1, LANE)),
                              jnp.broadcast_to(egl[:, hb:hb + 1], (1, LANE)))
            s_new = s2 * egl_p + jnp.where(bd_mask, _dot(kdec_t2.astype(BF16), vnew2.astype(BF16)), 0.0)
            s_scr[p] = s_new

            @pl.when(jnp.logical_not(is_bwd))
            def _():
                of_scr[of_rows, p * LANE:(p + 1) * LANE] = o2

            @pl.when(is_bwd)
            def _():
                o_tot = of_scr[of_rows, p * LANE:(p + 1) * LANE] + o2
                inv = _pair_rsqrt_norm(o_tot, RMS_EPS, 1.0 / GDN_DV)
                z2 = z_ref[rows, p * LANE:(p + 1) * LANE]
                o_ref[rows, p * LANE:(p + 1) * LANE] = (o_tot * inv * gn * (z2 * _sigmoid(z2))).astype(BF16)

        return carry

    n_iter = jnp.where(is_meta, 1, nchunk)
    lax.fori_loop(0, n_iter, chunk_body, 0)


def _gdn(qn, kn, vv, gates, z, qm, km, vm, gm, gn2, *, batch, seq, blk):
    nblk = seq // blk
    nchunk = blk // CHUNK
    rows = batch * seq

    def blk_of(t):
        return jnp.where(t > nblk, 2 * nblk - t, jnp.maximum(t - 1, 0))

    main_map = lambda b, t: (b * nblk + blk_of(t), 0)
    gate_map = lambda b, t: (b * nblk + blk_of(t), jnp.where(t > nblk, 1, 0))
    late_map = lambda b, t: (b * nblk + jnp.where(t > nblk, 2 * nblk - t, nblk - 1), 0)
    kernel = functools.partial(_gdn_kernel, nblk=nblk, nchunk=nchunk)
    return pl.pallas_call(
        kernel,
        out_shape=jax.ShapeDtypeStruct((rows, GDN_W), BF16),
        grid=(batch, 2 * nblk + 1),
        in_specs=[
            pl.BlockSpec((blk, GDN_W), main_map),
            pl.BlockSpec((blk, GDN_W), main_map),
            pl.BlockSpec((blk, GDN_W), main_map),
            pl.BlockSpec((blk, LANE), gate_map),
            pl.BlockSpec((blk, GDN_W), late_map),
            pl.BlockSpec((CHUNK, GDN_W), lambda b, t: (b, 0)),
            pl.BlockSpec((CHUNK, GDN_W), lambda b, t: (b, 0)),
            pl.BlockSpec((CHUNK, GDN_W), lambda b, t: (b, 0)),
            pl.BlockSpec((CHUNK, LANE), lambda b, t: (0, 0)),
            pl.BlockSpec((1, LANE), lambda b, t: (0, 0)),
        ],
        out_specs=pl.BlockSpec((blk, GDN_W), late_map),
        scratch_shapes=[
            pltpu.VMEM((N_PAIRS, LANE, LANE), F32),
            pltpu.VMEM((seq, GDN_W), F32),
        ],
        compiler_params=_cparams(("arbitrary", "arbitrary")),
        name="gated_delta",
    )(qn, kn, vv, gates, z, qm, km, vm, gm, gn2)


FF_SPLIT = 4


def _ffn_kernel(x_ref, mla_ref, gdn_ref, lng_ref, lnb_ref, wo_ref, l1g_ref, l1b_ref, w1_ref, w2_ref,
                l2g_ref, l2b_ref, o_ref):
    h = _layer_norm(x_ref[...], lng_ref[...], lnb_ref[...])
    mix = _dot(mla_ref[...], wo_ref[0:GDN_W, :]) + _dot(gdn_ref[...], wo_ref[GDN_W:2 * GDN_W, :])
    h1 = _layer_norm(DN_ALPHA * h + mix, l1g_ref[...], l1b_ref[...])
    hb = h1.astype(BF16)
    ffw = D_FF // FF_SPLIT
    acc = None
    for c in range(FF_SPLIT):
        t = jnp.maximum(_dot(hb, w1_ref[:, c * ffw:(c + 1) * ffw]), 0.0)
        part = _dot((t * t).astype(BF16), w2_ref[c * ffw:(c + 1) * ffw, :])
        acc = part if acc is None else acc + part
    o_ref[...] = _layer_norm(DN_ALPHA * h1 + acc, l2g_ref[...], l2b_ref[...])


def _ffn(x, mla, gdn, prm, *, tm):
    rows = x.shape[0]
    const = lambda i: (0, 0)
    single = dict(pipeline_mode=pl.Buffered(1))
    return pl.pallas_call(
        _ffn_kernel,
        out_shape=jax.ShapeDtypeStruct((rows, D_MODEL), F32),
        grid=(rows // tm,),
        in_specs=[
            pl.BlockSpec((tm, D_MODEL), lambda i: (i, 0)),
            pl.BlockSpec((tm, GDN_W), lambda i: (i, 0)),
            pl.BlockSpec((tm, GDN_W), lambda i: (i, 0)),
            pl.BlockSpec((1, D_MODEL), const),
            pl.BlockSpec((1, D_MODEL), const),
            pl.BlockSpec((D_MODEL, D_MODEL), const, **single),
            pl.BlockSpec((1, D_MODEL), const),
            pl.BlockSpec((1, D_MODEL), const),
            pl.BlockSpec((D_MODEL, D_FF), const, **single),
            pl.BlockSpec((D_FF, D_MODEL), const, **single),
            pl.BlockSpec((1, D_MODEL), const),
            pl.BlockSpec((1, D_MODEL), const),
        ],
        out_specs=pl.BlockSpec((tm, D_MODEL), lambda i: (i, 0)),
        compiler_params=_cparams(("arbitrary",)),
        name="outproj_ffn",
    )(x, mla, gdn, prm["ln_in_g"], prm["ln_in_b"], prm["w_out"], prm["ln1_g"], prm["ln1_b"], prm["w_ff1"],
      prm["w_ff2"], prm["ln2_g"], prm["ln2_b"])


def _pack_params(ln_in_g, ln_in_b, w_in, g_cq, g_ckv, w_uq, w_uk, w_uv, conv_w, a_log_f, a_log_b, dt_bias_f,
                 dt_bias_b, gdn_norm_g, w_out, ln1_g, ln1_b, w_ff1, w_ff2, ln2_g, ln2_b):
    l = 0
    w = w_in[l]
    o_kr = Q_LORA + KV_LORA
    o_qkv = o_kr + QK_ROPE
    o_z = o_qkv + QKV_W
    o_g = o_z + GDN_W
    zeros = lambda n: jnp.zeros((D_MODEL, n), F32)
    small = jnp.concatenate([w[:, o_g:o_g + 4 * HEADS], zeros(32), w[:, o_kr:o_kr + QK_ROPE], zeros(32)], axis=1)
    w_in_p = jnp.concatenate([w[:, :o_kr], small, w[:, o_qkv:o_z], w[:, o_z:o_g]], axis=1).astype(BF16)

    uq = w_uq[l].reshape(Q_LORA, HEADS, QK_NOPE + QK_ROPE)
    uq = jnp.pad(uq, ((0, 0), (0, 0), (0, LANE - QK_NOPE - QK_ROPE))).reshape(Q_LORA, HEADS * LANE)
    uk = w_uk[l].reshape(KV_LORA, HEADS, QK_NOPE)
    uk = jnp.pad(uk, ((0, 0), (0, 0), (0, LANE - QK_NOPE))).reshape(KV_LORA, HEADS * LANE)
    w_kv = jnp.concatenate([uk, w_uv[l]], axis=1)

    pad_to_lane = lambda v: jnp.pad(v, (0, LANE - v.shape[0])).reshape(1, LANE)
    row = lambda v: v.reshape(1, -1).astype(F32)
    return dict(
        ln_in_g=row(ln_in_g), ln_in_b=row(ln_in_b), w_in=w_in_p, g_cq=row(g_cq[l]), g_ckv=row(g_ckv[l]),
        w_uq=uq.astype(BF16), w_kv=w_kv.astype(BF16), conv_w=conv_w[l].astype(F32),
        alog=pad_to_lane(jnp.concatenate([a_log_f[l], a_log_b[l]]).astype(F32)),
        dtb=pad_to_lane(jnp.concatenate([dt_bias_f[l], dt_bias_b[l]]).astype(F32)),
        gn2=jnp.concatenate([gdn_norm_g[l], gdn_norm_g[l]]).reshape(1, LANE).astype(F32),
        w_out=w_out[l].astype(BF16), ln1_g=row(ln1_g[l]), ln1_b=row(ln1_b[l]), w_ff1=w_ff1[l].astype(BF16),
        w_ff2=w_ff2[l].astype(BF16), ln2_g=row(ln2_g[l]), ln2_b=row(ln2_b[l]),
    )


def _rope_tables(pos0, n):
    inv = ROPE_THETA ** (-jnp.arange(0, QK_ROPE, 2, dtype=F32) / QK_ROPE)
    ang = (pos0 + jnp.arange(n, dtype=F32))[:, None] * inv[None, :]
    cos, sin = jnp.cos(ang), jnp.sin(ang)
    half = QK_ROPE // 2
    z = lambda w: jnp.zeros((n, w), F32)
    c_rope = jnp.concatenate([z(QK_NOPE), cos, cos, z(LANE - QK_NOPE - QK_ROPE)], axis=1)
    s1 = jnp.concatenate([z(QK_NOPE), -sin, z(half), z(LANE - QK_NOPE - QK_ROPE)], axis=1)
    s2 = jnp.concatenate([z(QK_NOPE), z(half), sin, z(LANE - QK_NOPE - QK_ROPE)], axis=1)
    c_q = jnp.concatenate([jnp.ones((n, QK_NOPE), F32), cos, cos, z(LANE - QK_NOPE - QK_ROPE)], axis=1)
    scale = (QK_NOPE + QK_ROPE) ** -0.5
    return jnp.stack([c_q * scale, s1 * scale, s2 * scale, c_rope, s1, s2])


def _choose_tile(n, target):
    t = min(n, target)
    while n % t:
        t //= 2
    return t


def _encode_group(x, meta, prm):
    batch, seq, _ = x.shape
    rows = batch * seq
    xf = x.reshape(rows, D_MODEL)
    tm = _choose_tile(seq, ROW_TILE)
    q, k, v, qkv_pre, z, gates = _inproj(xf, _rope_tables(float(N_META), seq), prm, tm=tm, seq_blocks=seq // tm)

    meta_qkv = meta["qkv_pre"]
    blk = _choose_tile(seq, GDN_BLOCK)
    nblk = seq // blk
    r8 = blk // 8
    zeros8 = jnp.zeros((8, QKV_W), F32)
    qn, kn, vv = _conv(
        qkv_pre, qkv_pre, qkv_pre, meta_qkv, zeros8, prm["conv_w"], batch=batch, blk=blk, nblk=nblk,
        main_map=lambda b, i: (b * nblk + i, 0),
        prev_map=lambda b, i: (jnp.maximum((b * nblk + i) * r8 - 1, 0), 0),
        next_map=lambda b, i: (jnp.minimum((b * nblk + i + 1) * r8, rows // 8 - 1), 0),
        pfirst_map=lambda b, i: (N_META // 8 - 1, 0),
        nlast_map=lambda b, i: (0, 0))
    qm, km, vm = _conv(
        meta_qkv, meta_qkv, meta_qkv, zeros8, qkv_pre, prm["conv_w"], batch=batch, blk=N_META, nblk=1,
        main_map=lambda b, i: (0, 0), prev_map=lambda b, i: (0, 0), next_map=lambda b, i: (0, 0),
        pfirst_map=lambda b, i: (0, 0), nlast_map=lambda b, i: (b * (seq // 8), 0))
    front = CHUNK - N_META
    pad_meta = lambda a: jnp.pad(a.reshape(batch, N_META, GDN_W), ((0, 0), (front, 0), (0, 0))).reshape(
        batch * CHUNK, GDN_W)
    gm = jnp.pad(meta["gates"][:, :LANE], ((front, 0), (0, 0)))
    gdn_out = _gdn(qn, kn, vv, gates, z, pad_meta(qm), pad_meta(km), pad_meta(vm), gm, prm["gn2"], batch=batch,
                   seq=seq, blk=blk)

    mla_out = _attention(q, k, v, meta["k"], meta["v"], batch=batch, seq=seq, tq=_choose_tile(seq, Q_TILE))
    out = _ffn(xf, mla_out, gdn_out, prm, tm=tm)
    return out.reshape(batch, seq, D_MODEL)


def kernel(x_prompt, x_sample, meta_tokens, ln_in_g, ln_in_b, w_in, g_cq, g_ckv, w_uq, w_uk, w_uv, conv_w, a_log_f,
           a_log_b, dt_bias_f, dt_bias_b, gdn_norm_g, w_out, ln1_g, ln1_b, w_ff1, w_ff2, ln2_g, ln2_b):
    prm = _pack_params(ln_in_g, ln_in_b, w_in, g_cq, g_ckv, w_uq, w_uk, w_uv, conv_w, a_log_f, a_log_b, dt_bias_f,
                       dt_bias_b, gdn_norm_g, w_out, ln1_g, ln1_b, w_ff1, w_ff2, ln2_g, ln2_b)
    _, mk, mv, mqkv, _, mgates = _inproj(meta_tokens.astype(F32), _rope_tables(0.0, N_META), prm, tm=N_META,
                                         seq_blocks=1)
    meta = dict(k=mk, v=mv, qkv_pre=mqkv, gates=mgates)
    return (_encode_group(x_prompt, meta, prm), _encode_group(x_sample, meta, prm))
```

```python
import functools

import jax
import jax.numpy as jnp
from jax import lax
from jax.experimental import pallas as pl
from jax.experimental.pallas import tpu as pltpu

F32 = jnp.float32
BF16 = jnp.bfloat16

D_MODEL = 1024
N_META = 16
HEADS = 8
QK_NOPE = 64
QK_ROPE = 32
V_HEAD = 64
Q_LORA = 384
KV_LORA = 256
ROPE_THETA = 10000.0
GDN_DK = 64
GDN_DV = 64
CONV_K = 4
GDN_W = HEADS * GDN_DV
QKV_W = 3 * GDN_W
D_FF = 4 * D_MODEL
DEPTH = 1
DN_ALPHA = (2 * DEPTH) ** 0.25
LN_EPS = 1e-5
RMS_EPS = 1e-6
L2_EPS = 1e-6
LOG2_E = 1.4426950408889634

LANE = 128
CHUNK = 128
N_PAIRS = HEADS // 2
COL_CQ = 0
COL_CKV = COL_CQ + Q_LORA
COL_SMALL = COL_CKV + KV_LORA
COL_QKV = COL_SMALL + LANE
COL_Z = COL_QKV + QKV_W
N_IN_PACKED = COL_Z + GDN_W
VMEM_LIMIT = 56 * 1024 * 1024
ROW_TILE = 512
GDN_BLOCK = 1024
Q_TILE = 256
SCORE_ELEMS = 1 << 20
KEY_TILE = 1024
ATTN_HEADS = 4


def _cparams(sem):
    return pltpu.CompilerParams(dimension_semantics=sem, vmem_limit_bytes=VMEM_LIMIT)


def _layer_norm(x, g, b):
    mu = jnp.mean(x, axis=-1, keepdims=True)
    xc = x - mu
    var = jnp.mean(xc * xc, axis=-1, keepdims=True)
    return xc * lax.rsqrt(var + LN_EPS) * g + b


def _rms_norm(x, g):
    return x * lax.rsqrt(jnp.mean(x * x, axis=-1, keepdims=True) + RMS_EPS) * g


def _sigmoid(x):
    return 1.0 / (1.0 + jnp.exp(-x))


def _softplus(x):
    return jnp.maximum(x, 0.0) + jnp.log1p(jnp.exp(-jnp.abs(x)))


def _dot(a, b):
    return jnp.dot(a, b, preferred_element_type=F32)


def _inproj_kernel(x_ref, lng_ref, lnb_ref, win_ref, gcq_ref, gckv_ref, wuq_ref, wkv_ref, tab_ref,
                   alog_ref, dtb_ref, q_ref, k_ref, vt_ref, qkv_ref, z_ref, gate_ref, *, scan_len):
    tm = x_ref.shape[0]
    h = _layer_norm(x_ref[...], lng_ref[...], lnb_ref[...])
    proj = _dot(h.astype(BF16), win_ref[...])

    qkv_ref[...] = proj[:, COL_QKV:COL_QKV + QKV_W]
    z_ref[...] = proj[:, COL_Z:COL_Z + GDN_W]

    cqn = _rms_norm(proj[:, COL_CQ:COL_CQ + Q_LORA], gcq_ref[...])
    qfull = _dot(cqn.astype(BF16), wuq_ref[...])
    cq_t, s1q_t, s2q_t = tab_ref[0], tab_ref[1], tab_ref[2]
    for hd in range(HEADS):
        qh = qfull[:, hd * LANE:(hd + 1) * LANE]
        qr = qh * cq_t + pltpu.roll(qh, LANE - 16, 1) * s1q_t + pltpu.roll(qh, 16, 1) * s2q_t
        q_ref[:, hd * LANE:(hd + 1) * LANE] = qr.astype(BF16)

    small = proj[:, COL_SMALL:COL_SMALL + LANE]
    ck_t, s1k_t, s2k_t = tab_ref[3], tab_ref[4], tab_ref[5]
    krope = small * ck_t + pltpu.roll(small, LANE - 16, 1) * s1k_t + pltpu.roll(small, 16, 1) * s2k_t
    ckvn = _rms_norm(proj[:, COL_CKV:COL_CKV + KV_LORA], gckv_ref[...])
    kv = _dot(ckvn.astype(BF16), wkv_ref[...])
    for hd in range(HEADS):
        k_ref[:, hd * LANE:(hd + 1) * LANE] = (kv[:, hd * LANE:(hd + 1) * LANE] + krope).astype(BF16)
    vt_ref[...] = kv[:, HEADS * LANE:].T.astype(BF16)

    lane = lax.broadcasted_iota(jnp.int32, (tm, LANE), 1)
    row = lax.broadcasted_iota(jnp.int32, (tm, LANE), 0)
    gval = jnp.where(lane < 16, -jnp.exp(alog_ref[...]) * _softplus(small + dtb_ref[...]), _sigmoid(small))
    pos = row % scan_len
    pre = gval
    suf = gval
    sh = 1
    while sh < scan_len:
        pre = pre + jnp.where(pos >= sh, pltpu.roll(pre, sh, 0), 0.0)
        suf = suf + jnp.where(pos < scan_len - sh, pltpu.roll(suf, tm - sh, 0), 0.0)
        sh *= 2
    gate_ref[:, 0:LANE] = jnp.where(lane < 8, pre, gval)
    gate_ref[:, LANE:2 * LANE] = pltpu.roll(jnp.where((lane >= 8) & (lane < 16), suf, gval), LANE - 8, 1)


def _inproj(x, tabs, prm, *, tm, seq_blocks):
    rows = x.shape[0]
    n = rows // tm
    const = lambda i: (0, 0)
    kernel = functools.partial(_inproj_kernel, scan_len=min(tm, CHUNK))
    return pl.pallas_call(
        kernel,
        out_shape=(
            jax.ShapeDtypeStruct((rows, HEADS * LANE), BF16),
            jax.ShapeDtypeStruct((rows, HEADS * LANE), BF16),
            jax.ShapeDtypeStruct((HEADS * V_HEAD, rows), BF16),
            jax.ShapeDtypeStruct((rows, QKV_W), F32),
            jax.ShapeDtypeStruct((rows, GDN_W), F32),
            jax.ShapeDtypeStruct((rows, 2 * LANE), F32),
        ),
        grid=(n,),
        in_specs=[
            pl.BlockSpec((tm, D_MODEL), lambda i: (i, 0)),
            pl.BlockSpec((1, D_MODEL), const),
            pl.BlockSpec((1, D_MODEL), const),
            pl.BlockSpec((D_MODEL, N_IN_PACKED), const),
            pl.BlockSpec((1, Q_LORA), const),
            pl.BlockSpec((1, KV_LORA), const),
            pl.BlockSpec((Q_LORA, HEADS * LANE), const),
            pl.BlockSpec((KV_LORA, HEADS * LANE + HEADS * V_HEAD), const),
            pl.BlockSpec((6, tm, LANE), lambda i: (0, i % seq_blocks, 0)),
            pl.BlockSpec((1, LANE), const),
            pl.BlockSpec((1, LANE), const),
        ],
        out_specs=(
            pl.BlockSpec((tm, HEADS * LANE), lambda i: (i, 0)),
            pl.BlockSpec((tm, HEADS * LANE), lambda i: (i, 0)),
            pl.BlockSpec((HEADS * V_HEAD, tm), lambda i: (0, i)),
            pl.BlockSpec((tm, QKV_W), lambda i: (i, 0)),
            pl.BlockSpec((tm, GDN_W), lambda i: (i, 0)),
            pl.BlockSpec((tm, 2 * LANE), lambda i: (i, 0)),
        ),
        compiler_params=_cparams(("arbitrary",)),
        name="inproj",
    )(x, prm["ln_in_g"], prm["ln_in_b"], prm["w_in"], prm["g_cq"], prm["g_ckv"], prm["w_uq"], prm["w_kv"],
      tabs, prm["alog"], prm["dtb"])


def _pair_rsqrt_norm(x, eps, scale):
    lane = lax.broadcasted_iota(jnp.int32, x.shape, 1)
    left = lane < 64
    sq = x * x
    sa = jnp.sum(jnp.where(left, sq, 0.0), axis=-1, keepdims=True)
    sb = jnp.sum(jnp.where(left, 0.0, sq), axis=-1, keepdims=True)
    return jnp.where(left, lax.rsqrt(sa * scale + eps), lax.rsqrt(sb * scale + eps))


def _conv_kernel(x_ref, prev_ref, next_ref, pfirst_ref, nlast_ref, w_ref, q_ref, k_ref, v_ref):
    i = pl.program_id(1)
    nblk = pl.num_programs(1)
    blk = x_ref.shape[0]
    x = x_ref[...]
    prev = jnp.where(i == 0, pfirst_ref[...], prev_ref[...])
    nxt = jnp.where(i == nblk - 1, nlast_ref[...], next_ref[...])
    row = lax.broadcasted_iota(jnp.int32, x.shape, 0)
    xm1 = jnp.where(row == 0, prev[7:8, :], pltpu.roll(x, 1, 0))
    xp1 = jnp.where(row == blk - 1, nxt[0:1, :], pltpu.roll(x, blk - 1, 0))
    xp2 = jnp.where(row == blk - 2, nxt[0:1, :], jnp.where(row == blk - 1, nxt[1:2, :], pltpu.roll(x, blk - 2, 0)))
    w = w_ref[...]
    y = xm1 * w[0:1, :] + x * w[1:2, :] + xp1 * w[2:3, :] + xp2 * w[3:4, :]
    y = y * _sigmoid(y)
    for p in range(N_PAIRS):
        qp = y[:, p * LANE:(p + 1) * LANE]
        kp = y[:, GDN_W + p * LANE:GDN_W + (p + 1) * LANE]
        q_ref[:, p * LANE:(p + 1) * LANE] = qp * _pair_rsqrt_norm(qp, L2_EPS, 1.0) * (GDN_DK ** -0.5)
        k_ref[:, p * LANE:(p + 1) * LANE] = kp * _pair_rsqrt_norm(kp, L2_EPS, 1.0)
    v_ref[...] = y[:, 2 * GDN_W:]


def _conv(x, prev_arr, next_arr, pfirst, nlast, conv_w, *, batch, blk, nblk, main_map, prev_map, next_map,
          pfirst_map, nlast_map):
    rows = batch * nblk * blk
    return pl.pallas_call(
        _conv_kernel,
        out_shape=tuple(jax.ShapeDtypeStruct((rows, GDN_W), F32) for _ in range(3)),
        grid=(batch, nblk),
        in_specs=[
            pl.BlockSpec((blk, QKV_W), main_map),
            pl.BlockSpec((8, QKV_W), prev_map),
            pl.BlockSpec((8, QKV_W), next_map),
            pl.BlockSpec((8, QKV_W), pfirst_map),
            pl.BlockSpec((8, QKV_W), nlast_map),
            pl.BlockSpec((CONV_K, QKV_W), lambda b, i: (0, 0)),
        ],
        out_specs=tuple(pl.BlockSpec((blk, GDN_W), lambda b, i: (b * nblk + i, 0)) for _ in range(3)),
        compiler_params=_cparams(("arbitrary", "arbitrary")),
        name="conv_silu_l2",
    )(x, prev_arr, next_arr, pfirst, nlast, conv_w)


def _attn_kernel(q_ref, k_ref, vt_ref, km_ref, vmt_ref, o_ref):
    dn = (((1,), (1,)), ((), ()))
    heads = range(ATTN_HEADS)
    seq = k_ref.shape[0]
    tk = min(seq, KEY_TILE)
    qs = [q_ref[:, hh * LANE:(hh + 1) * LANE] for hh in heads]

    def scores(hh, j):
        return lax.dot_general(k_ref[j * tk:(j + 1) * tk, hh * LANE:(hh + 1) * LANE], qs[hh], dn,
                               preferred_element_type=F32)

    nk = seq // tk
    m_run, l_run, acc = [None] * ATTN_HEADS, [None] * ATTN_HEADS, [None] * ATTN_HEADS

    def update(hh, s, v_t):
        m_tile = jnp.max(s, axis=0, keepdims=True)
        m_new = m_tile if m_run[hh] is None else jnp.maximum(m_run[hh], m_tile)
        p = jnp.exp2(s - m_new)
        l_tile = jnp.sum(p, axis=0, keepdims=True)
        pv = _dot(v_t, p.astype(BF16))
        if m_run[hh] is None:
            l_run[hh], acc[hh] = l_tile, pv
        else:
            alpha = jnp.exp2(m_run[hh] - m_new)
            l_run[hh] = l_run[hh] * alpha + l_tile
            acc[hh] = acc[hh] * alpha + pv
        m_run[hh] = m_new

    st_next = [scores(hh, 0) for hh in heads]
    for j in range(nk):
        st_cur = st_next
        if j + 1 < nk:
            st_next = [scores(hh, j + 1) for hh in heads]
        else:
            st_next = [lax.dot_general(km_ref[:, hh * LANE:(hh + 1) * LANE], qs[hh], dn,
                                       preferred_element_type=F32) for hh in heads]
        for hh in heads:
            update(hh, st_cur[hh], vt_ref[hh * V_HEAD:(hh + 1) * V_HEAD, j * tk:(j + 1) * tk])
    for hh in heads:
        update(hh, st_next[hh], vmt_ref[hh * V_HEAD:(hh + 1) * V_HEAD, :])
    outs = [acc[hh] / l_run[hh] for hh in heads]
    for pr in range(ATTN_HEADS // 2):
        o_ref[:, pr * LANE:(pr + 1) * LANE] = jnp.concatenate(outs[2 * pr:2 * pr + 2], axis=0).T.astype(BF16)


def _attention(q, k, vt, km, vmt, *, batch, seq, tq):
    nq = seq // tq
    rows = batch * seq
    return pl.pallas_call(
        _attn_kernel,
        out_shape=jax.ShapeDtypeStruct((rows, HEADS * V_HEAD), BF16),
        grid=(batch, HEADS // ATTN_HEADS, nq),
        in_specs=[
            pl.BlockSpec((tq, ATTN_HEADS * LANE), lambda b, p, i: (b * nq + i, p)),
            pl.BlockSpec((seq, ATTN_HEADS * LANE), lambda b, p, i: (b, p)),
            pl.BlockSpec((ATTN_HEADS * V_HEAD, seq), lambda b, p, i: (p, b)),
            pl.BlockSpec((N_META, ATTN_HEADS * LANE), lambda b, p, i: (0, p)),
            pl.BlockSpec((ATTN_HEADS * V_HEAD, N_META), lambda b, p, i: (p, 0)),
        ],
        out_specs=pl.BlockSpec((tq, ATTN_HEADS * V_HEAD), lambda b, p, i: (b * nq + i, p)),
        compiler_params=_cparams(("arbitrary", "arbitrary", "arbitrary")),
        name="mla_attention",
    )(q, k, vt, km, vmt)


def _blockdiag_rows(x):
    lane = lax.broadcasted_iota(jnp.int32, x.shape, 1)
    zero = jnp.zeros_like(x)
    return jnp.concatenate([jnp.where(lane < 64, x, zero), jnp.where(lane < 64, zero, x)], axis=0)


def _gdn_kernel(q_ref, k_ref, v_ref, g_ref, z_ref, qm_ref, km_ref, vm_ref, gm_ref, gn_ref, o_ref,
                s_scr, of_scr, ob_scr, u_scr, wq_scr, attn_scr, kd_scr, egl_scr, *, nblk, nchunk, unroll):
    t = pl.program_id(1)
    is_meta = t == 0
    is_bwd = t > nblk
    blk_idx = jnp.where(is_bwd, 2 * nblk - t, jnp.maximum(t - 1, 0))
    blk = nchunk * CHUNK

    @pl.when(jnp.logical_or(t == 0, t == nblk + 1))
    def _():
        s_scr[...] = jnp.zeros_like(s_scr)

    row_i = lax.broadcasted_iota(jnp.int32, (CHUNK, CHUNK), 0)
    col_i = lax.broadcasted_iota(jnp.int32, (CHUNK, CHUNK), 1)
    sgn = jnp.where(is_bwd, -1, 1)
    d_rc = (col_i - row_i) * sgn
    incl = d_rc <= 0
    strict = d_rc < 0
    incl2 = jnp.concatenate([incl, incl], axis=1)
    strict2 = jnp.concatenate([strict, strict], axis=1)
    left = col_i < 64
    top = row_i < 64
    eye = (row_i == col_i).astype(F32)
    eye2 = jnp.concatenate([eye, eye], axis=1)
    bd_mask = top == left
    same_blk = [jnp.concatenate([(row_i >> k) == (col_i >> k)] * 2, axis=1) for k in range(1, 8)]
    gn = gn_ref[...]

    def colb(arr, lane_idx):
        return jnp.broadcast_to(arr[:, lane_idx:lane_idx + 1], (CHUNK, CHUNK))

    def rowb(arr, row_idx):
        return jnp.broadcast_to(arr[row_idx:row_idx + 1, :], (CHUNK, CHUNK))

    def bd2(x):
        zero = jnp.zeros((CHUNK, CHUNK), x.dtype)
        return jnp.concatenate([jnp.concatenate([x[:, :CHUNK], zero], axis=1),
                                jnp.concatenate([zero, x[:, CHUNK:]], axis=1)], axis=0)

    def load_chunk(c):
        rows = pl.ds(pl.multiple_of(c * CHUNK, CHUNK), CHUNK)
        is_meta_chunk = c == 0
        q_c = jnp.where(is_meta, jnp.where(is_meta_chunk, qm_ref[...], 0.0), q_ref[rows, :])
        k_c = jnp.where(is_meta, jnp.where(is_meta_chunk, km_ref[...], 0.0), k_ref[rows, :])
        v_c = jnp.where(is_meta, jnp.where(is_meta_chunk, vm_ref[...], 0.0), v_ref[rows, :])
        g_c = jnp.where(is_meta, jnp.where(is_meta_chunk, gm_ref[...], 0.0), g_ref[rows, :])
        g_t = g_c.T
        k_t = k_c.T
        gl_row = jnp.where(is_bwd, g_c[0:1, :], g_c[CHUNK - 1:CHUNK, :])
        gl_col = jnp.where(is_bwd, g_t[:, 0:1], g_t[:, CHUNK - 1:CHUNK])
        egc = jnp.exp(g_c)
        ekd_t = jnp.exp(gl_col - g_t)
        egl = jnp.exp(gl_row)
        return dict(c=c, q=q_c, k=k_c, v=v_c, g=g_c, g_t=g_t, k_t=k_t, egc=egc, ekd_t=ekd_t, egl=egl)

    def prep_stages(gi):
        chunks = [load_chunk(gi * unroll + j) for j in range(unroll)]
        chains = [dict(ch=ch, p=p) for ch in chunks for p in range(N_PAIRS)]

        for cn in chains:
            ch, p = cn["ch"], cn["p"]
            ha, hb = 2 * p, 2 * p + 1
            g_c, g_t = ch["g"], ch["g_t"]
            q2 = ch["q"][:, p * LANE:(p + 1) * LANE]
            k2 = ch["k"][:, p * LANE:(p + 1) * LANE]
            v2 = ch["v"][:, p * LANE:(p + 1) * LANE]
            kt2 = ch["k_t"][p * LANE:(p + 1) * LANE, :]
            gdiff2 = jnp.concatenate([colb(g_c, ha) - rowb(g_t, ha), colb(g_c, hb) - rowb(g_t, hb)], axis=1)
            dec2 = jnp.exp(jnp.where(incl2, gdiff2, -jnp.inf))
            beta_a = colb(g_c, 16 + ha)
            beta_b = colb(g_c, 16 + hb)
            beta_p = jnp.where(left, beta_a, beta_b)
            egc_p = jnp.where(left, colb(ch["egc"], ha), colb(ch["egc"], hb))
            zero_t = jnp.zeros_like(kt2)
            w_qk = jnp.concatenate([jnp.where(top, kt2, zero_t), jnp.where(top, zero_t, kt2)], axis=1)
            qkkk = _dot(jnp.concatenate([q2, k2], axis=0).astype(BF16), w_qk.astype(BF16))
            attn_scr[ch["c"], p] = (qkkk[:CHUNK] * dec2).astype(BF16)
            m2 = jnp.where(strict2, jnp.concatenate([beta_a, beta_b], axis=1) * qkkk[CHUNK:] * dec2, 0.0)
            cn["m2"] = m2
            cn["tinv"] = eye2 - jnp.where(same_blk[0], m2, 0.0)
            vb2 = v2 * beta_p
            kbe2 = k2 * beta_p * egc_p
            cn["w_rhs"] = jnp.concatenate([_blockdiag_rows(vb2), _blockdiag_rows(kbe2)], axis=1).astype(BF16)
            cn["qg"] = (q2 * egc_p).astype(BF16)
            ekd_rows = jnp.where(top, rowb(ch["ekd_t"], ha), rowb(ch["ekd_t"], hb))
            kd_scr[ch["c"], p] = (kt2 * ekd_rows).astype(BF16)
            egl_scr[ch["c"], p] = jnp.where(left[0:8, :], jnp.broadcast_to(ch["egl"][:, ha:ha + 1], (8, LANE)),
                                            jnp.broadcast_to(ch["egl"][:, hb:hb + 1], (8, LANE)))
        yield

        for k in range(1, len(same_blk)):
            for cn in chains:
                join = jnp.where(jnp.logical_and(same_blk[k], jnp.logical_not(same_blk[k - 1])), cn["m2"], 0.0)
                cn["tb"] = cn["tinv"].astype(BF16)
                cn["tc"] = _dot(cn["tb"], bd2(join.astype(BF16)))
            yield
            for cn in chains:
                cn["tinv"] = cn["tinv"] - _dot(cn["tc"].astype(BF16), bd2(cn["tb"]))
            yield

        for cn in chains:
            uw = _dot(cn["tinv"].astype(BF16), cn["w_rhs"])
            c, p = cn["ch"]["c"], cn["p"]
            u_scr[c, p] = uw[:, :LANE]
            wq_scr[c, p] = jnp.concatenate([uw[:, LANE:].astype(BF16), cn["qg"]], axis=0)

    def state_stages(gi):
        for j in range(unroll):
            c = gi * unroll + jnp.where(is_bwd, unroll - 1 - j, j)
            rows = pl.ds(pl.multiple_of(c * CHUNK, CHUNK), CHUNK)
            states = [s_scr[p] for p in range(N_PAIRS)]
            ws_qs = [_dot(wq_scr[c, p], states[p].astype(BF16)) for p in range(N_PAIRS)]
            yield
            vnew = [(u_scr[c, p] - ws_qs[p][:CHUNK]).astype(BF16) for p in range(N_PAIRS)]
            for p in range(N_PAIRS):
                ob_scr[rows, p * LANE:(p + 1) * LANE] = (ws_qs[p][CHUNK:]
                                                         + _dot(attn_scr[c, p], _blockdiag_rows(vnew[p])))
            for p in range(N_PAIRS):
                s_scr[p] = states[p] * egl_scr[c, p][0:1, :] + jnp.where(bd_mask, _dot(kd_scr[c, p], vnew[p]), 0.0)
            yield

    def run_interleaved(*gens):
        live = list(gens)
        while live:
            for g in list(live):
                try:
                    next(g)
                except StopIteration:
                    live.remove(g)

    ngroups = nchunk // unroll
    n_order = jnp.where(is_meta, 1, ngroups)

    def group_at(i):
        return jnp.where(is_bwd, ngroups - 1 - i, i)

    def skewed(i, carry):
        run_interleaved(prep_stages(group_at(i)), state_stages(group_at(i - 1)))
        return carry

    run_interleaved(prep_stages(group_at(0)))
    lax.fori_loop(1, n_order, skewed, 0)
    run_interleaved(state_stages(group_at(n_order - 1)))

    blk_rows = pl.ds(pl.multiple_of(blk_idx * blk, CHUNK), blk)

    @pl.when(jnp.logical_and(jnp.logical_not(is_bwd), jnp.logical_not(is_meta)))
    def _():
        of_scr[blk_rows, :] = ob_scr[...]

    @pl.when(is_bwd)
    def _():
        for p in range(N_PAIRS):
            o_tot = of_scr[blk_rows, p * LANE:(p + 1) * LANE] + ob_scr[:, p * LANE:(p + 1) * LANE]
            inv = _pair_rsqrt_norm(o_tot, RMS_EPS, 1.0 / GDN_DV)
            z2 = z_ref[:, p * LANE:(p + 1) * LANE]
            o_ref[:, p * LANE:(p + 1) * LANE] = (o_tot * inv * gn * (z2 * _sigmoid(z2))).astype(BF16)


def _gdn(qn, kn, vv, gates, z, qm, km, vm, gm, gn2, *, batch, seq, blk):
    nblk = seq // blk
    nchunk = blk // CHUNK
    rows = batch * seq

    def blk_of(t):
        return jnp.where(t > nblk, 2 * nblk - t, jnp.maximum(t - 1, 0))

    main_map = lambda b, t: (b * nblk + blk_of(t), 0)
    gate_map = lambda b, t: (b * nblk + blk_of(t), jnp.where(t > nblk, 1, 0))
    late_map = lambda b, t: (b * nblk + jnp.where(t > nblk, 2 * nblk - t, nblk - 1), 0)
    unroll = 2 if nchunk % 2 == 0 else 1
    kernel = functools.partial(_gdn_kernel, nblk=nblk, nchunk=nchunk, unroll=unroll)
    return pl.pallas_call(
        kernel,
        out_shape=jax.ShapeDtypeStruct((rows, GDN_W), BF16),
        grid=(batch, 2 * nblk + 1),
        in_specs=[
            pl.BlockSpec((blk, GDN_W), main_map),
            pl.BlockSpec((blk, GDN_W), main_map),
            pl.BlockSpec((blk, GDN_W), main_map),
            pl.BlockSpec((blk, LANE), gate_map),
            pl.BlockSpec((blk, GDN_W), late_map),
            pl.BlockSpec((CHUNK, GDN_W), lambda b, t: (b, 0)),
            pl.BlockSpec((CHUNK, GDN_W), lambda b, t: (b, 0)),
            pl.BlockSpec((CHUNK, GDN_W), lambda b, t: (b, 0)),
            pl.BlockSpec((CHUNK, LANE), lambda b, t: (0, 0)),
            pl.BlockSpec((1, LANE), lambda b, t: (0, 0)),
        ],
        out_specs=pl.BlockSpec((blk, GDN_W), late_map),
        scratch_shapes=[
            pltpu.VMEM((N_PAIRS, LANE, LANE), F32),
            pltpu.VMEM((seq, GDN_W), F32),
            pltpu.VMEM((blk, GDN_W), F32),
            pltpu.VMEM((nchunk, N_PAIRS, CHUNK, LANE), F32),
            pltpu.VMEM((nchunk, N_PAIRS, 2 * CHUNK, LANE), BF16),
            pltpu.VMEM((nchunk, N_PAIRS, CHUNK, 2 * CHUNK), BF16),
            pltpu.VMEM((nchunk, N_PAIRS, LANE, CHUNK), BF16),
            pltpu.VMEM((nchunk, N_PAIRS, 8, LANE), F32),
        ],
        compiler_params=_cparams(("arbitrary", "arbitrary")),
        name="gated_delta",
    )(qn, kn, vv, gates, z, qm, km, vm, gm, gn2)


FF_SPLIT = 4


def _ffn_kernel(x_ref, mla_ref, gdn_ref, lng_ref, lnb_ref, wo_ref, l1g_ref, l1b_ref, w1_ref, w2_ref,
                l2g_ref, l2b_ref, o_ref):
    h = _layer_norm(x_ref[...], lng_ref[...], lnb_ref[...])
    mix = _dot(mla_ref[...], wo_ref[0:GDN_W, :]) + _dot(gdn_ref[...], wo_ref[GDN_W:2 * GDN_W, :])
    h1 = _layer_norm(DN_ALPHA * h + mix, l1g_ref[...], l1b_ref[...])
    hb = h1.astype(BF16)
    ffw = D_FF // FF_SPLIT
    acc = None
    for c in range(FF_SPLIT):
        t = jnp.maximum(_dot(hb, w1_ref[:, c * ffw:(c + 1) * ffw]), 0.0)
        part = _dot((t * t).astype(BF16), w2_ref[c * ffw:(c + 1) * ffw, :])
        acc = part if acc is None else acc + part
    o_ref[...] = _layer_norm(DN_ALPHA * h1 + acc, l2g_ref[...], l2b_ref[...])


def _ffn(x, mla, gdn, prm, *, tm):
    rows = x.shape[0]
    const = lambda i: (0, 0)
    single = dict(pipeline_mode=pl.Buffered(1))
    return pl.pallas_call(
        _ffn_kernel,
        out_shape=jax.ShapeDtypeStruct((rows, D_MODEL), F32),
        grid=(rows // tm,),
        in_specs=[
            pl.BlockSpec((tm, D_MODEL), lambda i: (i, 0)),
            pl.BlockSpec((tm, GDN_W), lambda i: (i, 0)),
            pl.BlockSpec((tm, GDN_W), lambda i: (i, 0)),
            pl.BlockSpec((1, D_MODEL), const),
            pl.BlockSpec((1, D_MODEL), const),
            pl.BlockSpec((D_MODEL, D_MODEL), const, **single),
            pl.BlockSpec((1, D_MODEL), const),
            pl.BlockSpec((1, D_MODEL), const),
            pl.BlockSpec((D_MODEL, D_FF), const, **single),
            pl.BlockSpec((D_FF, D_MODEL), const, **single),
            pl.BlockSpec((1, D_MODEL), const),
            pl.BlockSpec((1, D_MODEL), const),
        ],
        out_specs=pl.BlockSpec((tm, D_MODEL), lambda i: (i, 0)),
        compiler_params=_cparams(("arbitrary",)),
        name="outproj_ffn",
    )(x, mla, gdn, prm["ln_in_g"], prm["ln_in_b"], prm["w_out"], prm["ln1_g"], prm["ln1_b"], prm["w_ff1"],
      prm["w_ff2"], prm["ln2_g"], prm["ln2_b"])


def _pack_params(ln_in_g, ln_in_b, w_in, g_cq, g_ckv, w_uq, w_uk, w_uv, conv_w, a_log_f, a_log_b, dt_bias_f,
                 dt_bias_b, gdn_norm_g, w_out, ln1_g, ln1_b, w_ff1, w_ff2, ln2_g, ln2_b):
    l = 0
    w = w_in[l]
    o_kr = Q_LORA + KV_LORA
    o_qkv = o_kr + QK_ROPE
    o_z = o_qkv + QKV_W
    o_g = o_z + GDN_W
    zeros = lambda n: jnp.zeros((D_MODEL, n), F32)
    small = jnp.concatenate([w[:, o_g:o_g + 4 * HEADS], zeros(32), w[:, o_kr:o_kr + QK_ROPE], zeros(32)], axis=1)
    w_in_p = jnp.concatenate([w[:, :o_kr], small, w[:, o_qkv:o_z], w[:, o_z:o_g]], axis=1).astype(BF16)

    uq = w_uq[l].reshape(Q_LORA, HEADS, QK_NOPE + QK_ROPE)
    uq = jnp.pad(uq, ((0, 0), (0, 0), (0, LANE - QK_NOPE - QK_ROPE))).reshape(Q_LORA, HEADS * LANE)
    uk = w_uk[l].reshape(KV_LORA, HEADS, QK_NOPE)
    uk = jnp.pad(uk, ((0, 0), (0, 0), (0, LANE - QK_NOPE))).reshape(KV_LORA, HEADS * LANE)
    w_kv = jnp.concatenate([uk, w_uv[l]], axis=1)

    pad_to_lane = lambda v: jnp.pad(v, (0, LANE - v.shape[0])).reshape(1, LANE)
    row = lambda v: v.reshape(1, -1).astype(F32)
    return dict(
        ln_in_g=row(ln_in_g), ln_in_b=row(ln_in_b), w_in=w_in_p, g_cq=row(g_cq[l]), g_ckv=row(g_ckv[l]),
        w_uq=uq.astype(BF16), w_kv=w_kv.astype(BF16), conv_w=conv_w[l].astype(F32),
        alog=pad_to_lane(jnp.concatenate([a_log_f[l], a_log_b[l]]).astype(F32)),
        dtb=pad_to_lane(jnp.concatenate([dt_bias_f[l], dt_bias_b[l]]).astype(F32)),
        gn2=jnp.concatenate([gdn_norm_g[l], gdn_norm_g[l]]).reshape(1, LANE).astype(F32),
        w_out=w_out[l].astype(BF16), ln1_g=row(ln1_g[l]), ln1_b=row(ln1_b[l]), w_ff1=w_ff1[l].astype(BF16),
        w_ff2=w_ff2[l].astype(BF16), ln2_g=row(ln2_g[l]), ln2_b=row(ln2_b[l]),
    )


def _rope_tables(pos0, n):
    inv = ROPE_THETA ** (-jnp.arange(0, QK_ROPE, 2, dtype=F32) / QK_ROPE)
    ang = (pos0 + jnp.arange(n, dtype=F32))[:, None] * inv[None, :]
    cos, sin = jnp.cos(ang), jnp.sin(ang)
    half = QK_ROPE // 2
    z = lambda w: jnp.zeros((n, w), F32)
    c_rope = jnp.concatenate([z(QK_NOPE), cos, cos, z(LANE - QK_NOPE - QK_ROPE)], axis=1)
    s1 = jnp.concatenate([z(QK_NOPE), -sin, z(half), z(LANE - QK_NOPE - QK_ROPE)], axis=1)
    s2 = jnp.concatenate([z(QK_NOPE), z(half), sin, z(LANE - QK_NOPE - QK_ROPE)], axis=1)
    c_q = jnp.concatenate([jnp.ones((n, QK_NOPE), F32), cos, cos, z(LANE - QK_NOPE - QK_ROPE)], axis=1)
    scale = (QK_NOPE + QK_ROPE) ** -0.5 * LOG2_E
    return jnp.stack([c_q * scale, s1 * scale, s2 * scale, c_rope, s1, s2])


def _choose_tile(n, target):
    t = min(n, target)
    while n % t:
        t //= 2
    return t


def _encode_group(x, meta, prm):
    batch, seq, _ = x.shape
    rows = batch * seq
    xf = x.reshape(rows, D_MODEL)
    tm = _choose_tile(seq, ROW_TILE)
    q, k, vt, qkv_pre, z, gates = _inproj(xf, _rope_tables(float(N_META), seq), prm, tm=tm, seq_blocks=seq // tm)

    meta_qkv = meta["qkv_pre"]
    blk = _choose_tile(seq, GDN_BLOCK)
    nblk = seq // blk
    r8 = blk // 8
    zeros8 = jnp.zeros((8, QKV_W), F32)
    qn, kn, vv = _conv(
        qkv_pre, qkv_pre, qkv_pre, meta_qkv, zeros8, prm["conv_w"], batch=batch, blk=blk, nblk=nblk,
        main_map=lambda b, i: (b * nblk + i, 0),
        prev_map=lambda b, i: (jnp.maximum((b * nblk + i) * r8 - 1, 0), 0),
        next_map=lambda b, i: (jnp.minimum((b * nblk + i + 1) * r8, rows // 8 - 1), 0),
        pfirst_map=lambda b, i: (N_META // 8 - 1, 0),
        nlast_map=lambda b, i: (0, 0))
    qm, km, vm = _conv(
        meta_qkv, meta_qkv, meta_qkv, zeros8, qkv_pre, prm["conv_w"], batch=batch, blk=N_META, nblk=1,
        main_map=lambda b, i: (0, 0), prev_map=lambda b, i: (0, 0), next_map=lambda b, i: (0, 0),
        pfirst_map=lambda b, i: (0, 0), nlast_map=lambda b, i: (b * (seq // 8), 0))
    front = CHUNK - N_META
    pad_meta = lambda a: jnp.pad(a.reshape(batch, N_META, GDN_W), ((0, 0), (front, 0), (0, 0))).reshape(
        batch * CHUNK, GDN_W)
    gm = jnp.pad(meta["gates"][:, :LANE], ((front, 0), (0, 0)))
    gdn_out = _gdn(qn, kn, vv, gates, z, pad_meta(qm), pad_meta(km), pad_meta(vm), gm, prm["gn2"], batch=batch,
                   seq=seq, blk=blk)

    tq = _choose_tile(seq, max(Q_TILE, SCORE_ELEMS // seq))
    mla_out = _attention(q, k, vt, meta["k"], meta["vt"], batch=batch, seq=seq, tq=tq)
    out = _ffn(xf, mla_out, gdn_out, prm, tm=tm)
    return out.reshape(batch, seq, D_MODEL)


def kernel(x_prompt, x_sample, meta_tokens, ln_in_g, ln_in_b, w_in, g_cq, g_ckv, w_uq, w_uk, w_uv, conv_w, a_log_f,
           a_log_b, dt_bias_f, dt_bias_b, gdn_norm_g, w_out, ln1_g, ln1_b, w_ff1, w_ff2, ln2_g, ln2_b):
    prm = _pack_params(ln_in_g, ln_in_b, w_in, g_cq, g_ckv, w_uq, w_uk, w_uv, conv_w, a_log_f, a_log_b, dt_bias_f,
                       dt_bias_b, gdn_norm_g, w_out, ln1_g, ln1_b, w_ff1, w_ff2, ln2_g, ln2_b)
    _, mk, mv, mqkv, _, mgates = _inproj(meta_tokens.astype(F32), _rope_tables(0.0, N_META), prm, tm=N_META,
                                         seq_blocks=1)
    meta = dict(k=mk, vt=mv, qkv_pre=mqkv, gates=mgates)
    return (_encode_group(x_prompt, meta, prm), _encode_group(x_sample, meta, prm))
```

```python
import functools

import jax
import jax.numpy as jnp
from jax import lax
from jax.experimental import pallas as pl
from jax.experimental.pallas import tpu as pltpu

F32 = jnp.float32
BF16 = jnp.bfloat16

D_MODEL = 1024
N_META = 16
HEADS = 8
QK_NOPE = 64
QK_ROPE = 32
V_HEAD = 64
Q_LORA = 384
KV_LORA = 256
ROPE_THETA = 10000.0
GDN_DK = 64
GDN_DV = 64
CONV_K = 4
GDN_W = HEADS * GDN_DV
QKV_W = 3 * GDN_W
D_FF = 4 * D_MODEL
DEPTH = 1
DN_ALPHA = (2 * DEPTH) ** 0.25
LN_EPS = 1e-5
RMS_EPS = 1e-6
L2_EPS = 1e-6
LOG2_E = 1.4426950408889634

LANE = 128
CHUNK = 128
N_PAIRS = HEADS // 2
COL_CQ = 0
COL_CKV = COL_CQ + Q_LORA
COL_SMALL = COL_CKV + KV_LORA
COL_QKV = COL_SMALL + LANE
COL_Z = COL_QKV + QKV_W
N_IN_PACKED = COL_Z + GDN_W
VMEM_LIMIT = 56 * 1024 * 1024
ROW_TILE = 512
SUB_TILES = 2
GDN_BLOCK = 1024
Q_TILE = 256
SCORE_ELEMS = 1 << 20
KEY_TILE = 1024
ATTN_HEADS = 8


def _cparams(sem):
    return pltpu.CompilerParams(dimension_semantics=sem, vmem_limit_bytes=VMEM_LIMIT)


def _layer_norm(x, g, b):
    mu = jnp.mean(x, axis=-1, keepdims=True)
    xc = x - mu
    var = jnp.mean(xc * xc, axis=-1, keepdims=True)
    return xc * lax.rsqrt(var + LN_EPS) * g + b


def _rms_norm(x, g):
    return x * lax.rsqrt(jnp.mean(x * x, axis=-1, keepdims=True) + RMS_EPS) * g


def _sigmoid(x):
    return 1.0 / (1.0 + jnp.exp(-x))


def _softplus(x):
    return jnp.maximum(x, 0.0) + jnp.log1p(jnp.exp(-jnp.abs(x)))


def _dot(a, b):
    return jnp.dot(a, b, preferred_element_type=F32)


def _inproj_kernel(x_ref, lng_ref, lnb_ref, win_ref, gcq_ref, gckv_ref, wuq_ref, wkv_ref, tab_ref,
                   alog_ref, dtb_ref, q_ref, k_ref, vt_ref, qkv_ref, z_ref, gate_ref, *, scan_len):
    tm = x_ref.shape[0]
    h = _layer_norm(x_ref[...], lng_ref[...], lnb_ref[...])
    proj = _dot(h.astype(BF16), win_ref[...])

    qkv_ref[...] = proj[:, COL_QKV:COL_QKV + QKV_W]
    z_ref[...] = proj[:, COL_Z:COL_Z + GDN_W]

    cqn = _rms_norm(proj[:, COL_CQ:COL_CQ + Q_LORA], gcq_ref[...])
    qfull = _dot(cqn.astype(BF16), wuq_ref[...])
    cq_t, s1q_t, s2q_t = tab_ref[0], tab_ref[1], tab_ref[2]
    for hd in range(HEADS):
        qh = qfull[:, hd * LANE:(hd + 1) * LANE]
        qr = qh * cq_t + pltpu.roll(qh, LANE - 16, 1) * s1q_t + pltpu.roll(qh, 16, 1) * s2q_t
        q_ref[:, hd * LANE:(hd + 1) * LANE] = qr.astype(BF16)

    small = proj[:, COL_SMALL:COL_SMALL + LANE]
    ck_t, s1k_t, s2k_t = tab_ref[3], tab_ref[4], tab_ref[5]
    krope = small * ck_t + pltpu.roll(small, LANE - 16, 1) * s1k_t + pltpu.roll(small, 16, 1) * s2k_t
    ckvn = _rms_norm(proj[:, COL_CKV:COL_CKV + KV_LORA], gckv_ref[...])
    kv = _dot(ckvn.astype(BF16), wkv_ref[...])
    for hd in range(HEADS):
        k_ref[:, hd * LANE:(hd + 1) * LANE] = (kv[:, hd * LANE:(hd + 1) * LANE] + krope).astype(BF16)
    vt_ref[...] = kv[:, HEADS * LANE:].T.astype(BF16)

    lane = lax.broadcasted_iota(jnp.int32, (tm, LANE), 1)
    row = lax.broadcasted_iota(jnp.int32, (tm, LANE), 0)
    gval = jnp.where(lane < 16, -jnp.exp(alog_ref[...]) * _softplus(small + dtb_ref[...]), _sigmoid(small))
    pos = row % scan_len
    pre = gval
    suf = gval
    sh = 1
    while sh < scan_len:
        pre = pre + jnp.where(pos >= sh, pltpu.roll(pre, sh, 0), 0.0)
        suf = suf + jnp.where(pos < scan_len - sh, pltpu.roll(suf, tm - sh, 0), 0.0)
        sh *= 2
    gate_ref[:, 0:LANE] = jnp.where(lane < 8, pre, gval)
    gate_ref[:, LANE:2 * LANE] = pltpu.roll(jnp.where((lane >= 8) & (lane < 16), suf, gval), LANE - 8, 1)


def _inproj(x, tabs, prm, *, tm, seq_blocks):
    rows = x.shape[0]
    n = rows // tm
    const = lambda i: (0, 0)
    kernel = functools.partial(_inproj_kernel, scan_len=min(tm, CHUNK))
    return pl.pallas_call(
        kernel,
        out_shape=(
            jax.ShapeDtypeStruct((rows, HEADS * LANE), BF16),
            jax.ShapeDtypeStruct((rows, HEADS * LANE), BF16),
            jax.ShapeDtypeStruct((HEADS * V_HEAD, rows), BF16),
            jax.ShapeDtypeStruct((rows, QKV_W), F32),
            jax.ShapeDtypeStruct((rows, GDN_W), F32),
            jax.ShapeDtypeStruct((rows, 2 * LANE), F32),
        ),
        grid=(n,),
        in_specs=[
            pl.BlockSpec((tm, D_MODEL), lambda i: (i, 0)),
            pl.BlockSpec((1, D_MODEL), const),
            pl.BlockSpec((1, D_MODEL), const),
            pl.BlockSpec((D_MODEL, N_IN_PACKED), const),
            pl.BlockSpec((1, Q_LORA), const),
            pl.BlockSpec((1, KV_LORA), const),
            pl.BlockSpec((Q_LORA, HEADS * LANE), const),
            pl.BlockSpec((KV_LORA, HEADS * LANE + HEADS * V_HEAD), const),
            pl.BlockSpec((6, tm, LANE), lambda i: (0, i % seq_blocks, 0)),
            pl.BlockSpec((1, LANE), const),
            pl.BlockSpec((1, LANE), const),
        ],
        out_specs=(
            pl.BlockSpec((tm, HEADS * LANE), lambda i: (i, 0)),
            pl.BlockSpec((tm, HEADS * LANE), lambda i: (i, 0)),
            pl.BlockSpec((HEADS * V_HEAD, tm), lambda i: (0, i)),
            pl.BlockSpec((tm, QKV_W), lambda i: (i, 0)),
            pl.BlockSpec((tm, GDN_W), lambda i: (i, 0)),
            pl.BlockSpec((tm, 2 * LANE), lambda i: (i, 0)),
        ),
        compiler_params=_cparams(("arbitrary",)),
        name="inproj",
    )(x, prm["ln_in_g"], prm["ln_in_b"], prm["w_in"], prm["g_cq"], prm["g_ckv"], prm["w_uq"], prm["w_kv"],
      tabs, prm["alog"], prm["dtb"])


def _pair_rsqrt_norm(x, eps, scale):
    lane = lax.broadcasted_iota(jnp.int32, x.shape, 1)
    left = lane < 64
    sq = x * x
    sa = jnp.sum(jnp.where(left, sq, 0.0), axis=-1, keepdims=True)
    sb = jnp.sum(jnp.where(left, 0.0, sq), axis=-1, keepdims=True)
    return jnp.where(left, lax.rsqrt(sa * scale + eps), lax.rsqrt(sb * scale + eps))


def _conv_kernel(x_ref, prev_ref, next_ref, pfirst_ref, nlast_ref, w_ref, q_ref, k_ref, v_ref):
    i = pl.program_id(1)
    nblk = pl.num_programs(1)
    blk = x_ref.shape[0]
    x = x_ref[...]
    prev = jnp.where(i == 0, pfirst_ref[...], prev_ref[...])
    nxt = jnp.where(i == nblk - 1, nlast_ref[...], next_ref[...])
    row = lax.broadcasted_iota(jnp.int32, x.shape, 0)
    xm1 = jnp.where(row == 0, prev[7:8, :], pltpu.roll(x, 1, 0))
    xp1 = jnp.where(row == blk - 1, nxt[0:1, :], pltpu.roll(x, blk - 1, 0))
    xp2 = jnp.where(row == blk - 2, nxt[0:1, :], jnp.where(row == blk - 1, nxt[1:2, :], pltpu.roll(x, blk - 2, 0)))
    w = w_ref[...]
    y = xm1 * w[0:1, :] + x * w[1:2, :] + xp1 * w[2:3, :] + xp2 * w[3:4, :]
    y = y * _sigmoid(y)
    for p in range(N_PAIRS):
        qp = y[:, p * LANE:(p + 1) * LANE]
        kp = y[:, GDN_W + p * LANE:GDN_W + (p + 1) * LANE]
        q_ref[:, p * LANE:(p + 1) * LANE] = qp * _pair_rsqrt_norm(qp, L2_EPS, 1.0) * (GDN_DK ** -0.5)
        k_ref[:, p * LANE:(p + 1) * LANE] = kp * _pair_rsqrt_norm(kp, L2_EPS, 1.0)
    v_ref[...] = y[:, 2 * GDN_W:]


def _conv(x, prev_arr, next_arr, pfirst, nlast, conv_w, *, batch, blk, nblk, main_map, prev_map, next_map,
          pfirst_map, nlast_map):
    rows = batch * nblk * blk
    return pl.pallas_call(
        _conv_kernel,
        out_shape=tuple(jax.ShapeDtypeStruct((rows, GDN_W), F32) for _ in range(3)),
        grid=(batch, nblk),
        in_specs=[
            pl.BlockSpec((blk, QKV_W), main_map),
            pl.BlockSpec((8, QKV_W), prev_map),
            pl.BlockSpec((8, QKV_W), next_map),
            pl.BlockSpec((8, QKV_W), pfirst_map),
            pl.BlockSpec((8, QKV_W), nlast_map),
            pl.BlockSpec((CONV_K, QKV_W), lambda b, i: (0, 0)),
        ],
        out_specs=tuple(pl.BlockSpec((blk, GDN_W), lambda b, i: (b * nblk + i, 0)) for _ in range(3)),
        compiler_params=_cparams(("arbitrary", "arbitrary")),
        name="conv_silu_l2",
    )(x, prev_arr, next_arr, pfirst, nlast, conv_w)


def _attn_kernel(q_ref, k_ref, vt_ref, km_ref, vmt_ref, o_ref):
    dn = (((1,), (1,)), ((), ()))
    heads = range(ATTN_HEADS)
    seq = k_ref.shape[0]
    tk = min(seq, KEY_TILE)
    qs = [q_ref[:, hh * LANE:(hh + 1) * LANE] for hh in heads]

    def scores(hh, j):
        return lax.dot_general(k_ref[j * tk:(j + 1) * tk, hh * LANE:(hh + 1) * LANE], qs[hh], dn,
                               preferred_element_type=F32)

    nk = seq // tk
    m_run, l_run, acc = [None] * ATTN_HEADS, [None] * ATTN_HEADS, [None] * ATTN_HEADS

    def update(hh, s, v_t):
        m_tile = jnp.max(s, axis=0, keepdims=True)
        m_new = m_tile if m_run[hh] is None else jnp.maximum(m_run[hh], m_tile)
        p = jnp.exp2(s - m_new)
        l_tile = jnp.sum(p, axis=0, keepdims=True)
        pv = _dot(v_t, p.astype(BF16))
        if m_run[hh] is None:
            l_run[hh], acc[hh] = l_tile, pv
        else:
            alpha = jnp.exp2(m_run[hh] - m_new)
            l_run[hh] = l_run[hh] * alpha + l_tile
            acc[hh] = acc[hh] * alpha + pv
        m_run[hh] = m_new

    st_next = [scores(hh, 0) for hh in heads]
    for j in range(nk):
        st_cur = st_next
        if j + 1 < nk:
            st_next = [scores(hh, j + 1) for hh in heads]
        else:
            st_next = [lax.dot_general(km_ref[:, hh * LANE:(hh + 1) * LANE], qs[hh], dn,
                                       preferred_element_type=F32) for hh in heads]
        for hh in heads:
            update(hh, st_cur[hh], vt_ref[hh * V_HEAD:(hh + 1) * V_HEAD, j * tk:(j + 1) * tk])
    for hh in heads:
        update(hh, st_next[hh], vmt_ref[hh * V_HEAD:(hh + 1) * V_HEAD, :])
    outs = [acc[hh] / l_run[hh] for hh in heads]
    for pr in range(ATTN_HEADS // 2):
        o_ref[:, pr * LANE:(pr + 1) * LANE] = jnp.concatenate(outs[2 * pr:2 * pr + 2], axis=0).T.astype(BF16)


def _attention(q, k, vt, km, vmt, *, batch, seq, tq):
    nq = seq // tq
    rows = batch * seq
    return pl.pallas_call(
        _attn_kernel,
        out_shape=jax.ShapeDtypeStruct((rows, HEADS * V_HEAD), BF16),
        grid=(batch, HEADS // ATTN_HEADS, nq),
        in_specs=[
            pl.BlockSpec((tq, ATTN_HEADS * LANE), lambda b, p, i: (b * nq + i, p)),
            pl.BlockSpec((seq, ATTN_HEADS * LANE), lambda b, p, i: (b, p)),
            pl.BlockSpec((ATTN_HEADS * V_HEAD, seq), lambda b, p, i: (p, b)),
            pl.BlockSpec((N_META, ATTN_HEADS * LANE), lambda b, p, i: (0, p)),
            pl.BlockSpec((ATTN_HEADS * V_HEAD, N_META), lambda b, p, i: (p, 0)),
        ],
        out_specs=pl.BlockSpec((tq, ATTN_HEADS * V_HEAD), lambda b, p, i: (b * nq + i, p)),
        compiler_params=_cparams(("arbitrary", "arbitrary", "arbitrary")),
        name="mla_attention",
    )(q, k, vt, km, vmt)


def _blockdiag_rows(x):
    lane = lax.broadcasted_iota(jnp.int32, x.shape, 1)
    zero = jnp.zeros_like(x)
    return jnp.concatenate([jnp.where(lane < 64, x, zero), jnp.where(lane < 64, zero, x)], axis=0)


def _gdn_kernel(q_ref, k_ref, v_ref, g_ref, z_ref, qm_ref, km_ref, vm_ref, gm_ref, gn_ref, o_ref,
                s_scr, of_scr, ob_scr, u_scr, wq_scr, attn_scr, kd_scr, egl_scr, *, nblk, nchunk, unroll):
    t = pl.program_id(1)
    is_meta = t == 0
    is_bwd = t > nblk
    blk_idx = jnp.where(is_bwd, 2 * nblk - t, jnp.maximum(t - 1, 0))
    blk = nchunk * CHUNK

    @pl.when(jnp.logical_or(t == 0, t == nblk + 1))
    def _():
        s_scr[...] = jnp.zeros_like(s_scr)

    row_i = lax.broadcasted_iota(jnp.int32, (CHUNK, CHUNK), 0)
    col_i = lax.broadcasted_iota(jnp.int32, (CHUNK, CHUNK), 1)
    sgn = jnp.where(is_bwd, -1, 1)
    d_rc = (col_i - row_i) * sgn
    incl = d_rc <= 0
    strict = d_rc < 0
    incl2 = jnp.concatenate([incl, incl], axis=1)
    strict2 = jnp.concatenate([strict, strict], axis=1)
    left = col_i < 64
    top = row_i < 64
    eye = (row_i == col_i).astype(F32)
    eye2 = jnp.concatenate([eye, eye], axis=1)
    bd_mask = top == left
    same_blk = [jnp.concatenate([(row_i >> k) == (col_i >> k)] * 2, axis=1) for k in range(1, 8)]
    gn = gn_ref[...]

    def colb(arr, lane_idx):
        return jnp.broadcast_to(arr[:, lane_idx:lane_idx + 1], (CHUNK, CHUNK))

    def rowb(arr, row_idx):
        return jnp.broadcast_to(arr[row_idx:row_idx + 1, :], (CHUNK, CHUNK))

    def bd2(x):
        zero = jnp.zeros((CHUNK, CHUNK), x.dtype)
        return jnp.concatenate([jnp.concatenate([x[:, :CHUNK], zero], axis=1),
                                jnp.concatenate([zero, x[:, CHUNK:]], axis=1)], axis=0)

    def load_chunk(c):
        rows = pl.ds(pl.multiple_of(c * CHUNK, CHUNK), CHUNK)
        is_meta_chunk = c == 0
        q_c = jnp.where(is_meta, jnp.where(is_meta_chunk, qm_ref[...], 0.0), q_ref[rows, :])
        k_c = jnp.where(is_meta, jnp.where(is_meta_chunk, km_ref[...], 0.0), k_ref[rows, :])
        v_c = jnp.where(is_meta, jnp.where(is_meta_chunk, vm_ref[...], 0.0), v_ref[rows, :])
        g_c = jnp.where(is_meta, jnp.where(is_meta_chunk, gm_ref[...], 0.0), g_ref[rows, :])
        g_t = g_c.T
        k_t = k_c.T
        gl_row = jnp.where(is_bwd, g_c[0:1, :], g_c[CHUNK - 1:CHUNK, :])
        gl_col = jnp.where(is_bwd, g_t[:, 0:1], g_t[:, CHUNK - 1:CHUNK])
        egc = jnp.exp(g_c)
        ekd_t = jnp.exp(gl_col - g_t)
        egl = jnp.exp(gl_row)
        return dict(c=c, q=q_c, k=k_c, v=v_c, g=g_c, g_t=g_t, k_t=k_t, egc=egc, ekd_t=ekd_t, egl=egl)

    def prep_stages(gi):
        chunks = [load_chunk(gi * unroll + j) for j in range(unroll)]
        chains = [dict(ch=ch, p=p) for ch in chunks for p in range(N_PAIRS)]

        for cn in chains:
            ch, p = cn["ch"], cn["p"]
            ha, hb = 2 * p, 2 * p + 1
            g_c, g_t = ch["g"], ch["g_t"]
            q2 = ch["q"][:, p * LANE:(p + 1) * LANE]
            k2 = ch["k"][:, p * LANE:(p + 1) * LANE]
            v2 = ch["v"][:, p * LANE:(p + 1) * LANE]
            kt2 = ch["k_t"][p * LANE:(p + 1) * LANE, :]
            gdiff2 = jnp.concatenate([colb(g_c, ha) - rowb(g_t, ha), colb(g_c, hb) - rowb(g_t, hb)], axis=1)
            dec2 = jnp.exp(jnp.where(incl2, gdiff2, -jnp.inf))
            beta_a = colb(g_c, 16 + ha)
            beta_b = colb(g_c, 16 + hb)
            beta_p = jnp.where(left, beta_a, beta_b)
            egc_p = jnp.where(left, colb(ch["egc"], ha), colb(ch["egc"], hb))
            zero_t = jnp.zeros_like(kt2)
            w_qk = jnp.concatenate([jnp.where(top, kt2, zero_t), jnp.where(top, zero_t, kt2)], axis=1)
            qkkk = _dot(jnp.concatenate([q2, k2], axis=0).astype(BF16), w_qk.astype(BF16))
            attn_scr[ch["c"], p] = (qkkk[:CHUNK] * dec2).astype(BF16)
            m2 = jnp.where(strict2, jnp.concatenate([beta_a, beta_b], axis=1) * qkkk[CHUNK:] * dec2, 0.0)
            cn["m2"] = m2
            cn["tinv"] = eye2 - jnp.where(same_blk[0], m2, 0.0)
            vb2 = v2 * beta_p
            kbe2 = k2 * beta_p * egc_p
            cn["w_rhs"] = jnp.concatenate([_blockdiag_rows(vb2), _blockdiag_rows(kbe2)], axis=1).astype(BF16)
            cn["qg"] = (q2 * egc_p).astype(BF16)
            ekd_rows = jnp.where(top, rowb(ch["ekd_t"], ha), rowb(ch["ekd_t"], hb))
            kd_scr[ch["c"], p] = (kt2 * ekd_rows).astype(BF16)
            egl_scr[ch["c"], p] = jnp.where(left[0:8, :], jnp.broadcast_to(ch["egl"][:, ha:ha + 1], (8, LANE)),
                                            jnp.broadcast_to(ch["egl"][:, hb:hb + 1], (8, LANE)))
        yield

        for k in range(1, len(same_blk)):
            for cn in chains:
                join = jnp.where(jnp.logical_and(same_blk[k], jnp.logical_not(same_blk[k - 1])), cn["m2"], 0.0)
                cn["tb"] = cn["tinv"].astype(BF16)
                cn["tc"] = _dot(cn["tb"], bd2(join.astype(BF16)))
            yield
            for cn in chains:
                cn["tinv"] = cn["tinv"] - _dot(cn["tc"].astype(BF16), bd2(cn["tb"]))
            yield

        for cn in chains:
            uw = _dot(cn["tinv"].astype(BF16), cn["w_rhs"])
            c, p = cn["ch"]["c"], cn["p"]
            u_scr[c, p] = uw[:, :LANE]
            wq_scr[c, p] = jnp.concatenate([uw[:, LANE:].astype(BF16), cn["qg"]], axis=0)

    def state_stages(gi):
        for j in range(unroll):
            c = gi * unroll + jnp.where(is_bwd, unroll - 1 - j, j)
            rows = pl.ds(pl.multiple_of(c * CHUNK, CHUNK), CHUNK)
            states = [s_scr[p] for p in range(N_PAIRS)]
            ws_qs = [_dot(wq_scr[c, p], states[p].astype(BF16)) for p in range(N_PAIRS)]
            yield
            vnew = [(u_scr[c, p] - ws_qs[p][:CHUNK]).astype(BF16) for p in range(N_PAIRS)]
            for p in range(N_PAIRS):
                ob_scr[rows, p * LANE:(p + 1) * LANE] = (ws_qs[p][CHUNK:]
                                                         + _dot(attn_scr[c, p], _blockdiag_rows(vnew[p])))
            for p in range(N_PAIRS):
                s_scr[p] = states[p] * egl_scr[c, p][0:1, :] + jnp.where(bd_mask, _dot(kd_scr[c, p], vnew[p]), 0.0)
            yield

    def run_interleaved(*gens):
        live = list(gens)
        while live:
            for g in list(live):
                try:
                    next(g)
                except StopIteration:
                    live.remove(g)

    ngroups = nchunk // unroll
    n_order = jnp.where(is_meta, 1, ngroups)

    def group_at(i):
        return jnp.where(is_bwd, ngroups - 1 - i, i)

    def skewed(i, carry):
        run_interleaved(prep_stages(group_at(i)), state_stages(group_at(i - 1)))
        return carry

    run_interleaved(prep_stages(group_at(0)))
    lax.fori_loop(1, n_order, skewed, 0)
    run_interleaved(state_stages(group_at(n_order - 1)))

    blk_rows = pl.ds(pl.multiple_of(blk_idx * blk, CHUNK), blk)

    @pl.when(jnp.logical_and(jnp.logical_not(is_bwd), jnp.logical_not(is_meta)))
    def _():
        of_scr[blk_rows, :] = ob_scr[...]

    @pl.when(is_bwd)
    def _():
        for p in range(N_PAIRS):
            o_tot = of_scr[blk_rows, p * LANE:(p + 1) * LANE] + ob_scr[:, p * LANE:(p + 1) * LANE]
            inv = _pair_rsqrt_norm(o_tot, RMS_EPS, 1.0 / GDN_DV)
            z2 = z_ref[:, p * LANE:(p + 1) * LANE]
            o_ref[:, p * LANE:(p + 1) * LANE] = (o_tot * inv * gn * (z2 * _sigmoid(z2))).astype(BF16)


def _gdn(qn, kn, vv, gates, z, qm, km, vm, gm, gn2, *, batch, seq, blk):
    nblk = seq // blk
    nchunk = blk // CHUNK
    rows = batch * seq

    def blk_of(t):
        return jnp.where(t > nblk, 2 * nblk - t, jnp.maximum(t - 1, 0))

    main_map = lambda b, t: (b * nblk + blk_of(t), 0)
    gate_map = lambda b, t: (b * nblk + blk_of(t), jnp.where(t > nblk, 1, 0))
    late_map = lambda b, t: (b * nblk + jnp.where(t > nblk, 2 * nblk - t, nblk - 1), 0)
    unroll = 2 if nchunk % 2 == 0 else 1
    kernel = functools.partial(_gdn_kernel, nblk=nblk, nchunk=nchunk, unroll=unroll)
    return pl.pallas_call(
        kernel,
        out_shape=jax.ShapeDtypeStruct((rows, GDN_W), BF16),
        grid=(batch, 2 * nblk + 1),
        in_specs=[
            pl.BlockSpec((blk, GDN_W), main_map),
            pl.BlockSpec((blk, GDN_W), main_map),
            pl.BlockSpec((blk, GDN_W), main_map),
            pl.BlockSpec((blk, LANE), gate_map),
            pl.BlockSpec((blk, GDN_W), late_map),
            pl.BlockSpec((CHUNK, GDN_W), lambda b, t: (b, 0)),
            pl.BlockSpec((CHUNK, GDN_W), lambda b, t: (b, 0)),
            pl.BlockSpec((CHUNK, GDN_W), lambda b, t: (b, 0)),
            pl.BlockSpec((CHUNK, LANE), lambda b, t: (0, 0)),
            pl.BlockSpec((1, LANE), lambda b, t: (0, 0)),
        ],
        out_specs=pl.BlockSpec((blk, GDN_W), late_map),
        scratch_shapes=[
            pltpu.VMEM((N_PAIRS, LANE, LANE), F32),
            pltpu.VMEM((seq, GDN_W), F32),
            pltpu.VMEM((blk, GDN_W), F32),
            pltpu.VMEM((nchunk, N_PAIRS, CHUNK, LANE), F32),
            pltpu.VMEM((nchunk, N_PAIRS, 2 * CHUNK, LANE), BF16),
            pltpu.VMEM((nchunk, N_PAIRS, CHUNK, 2 * CHUNK), BF16),
            pltpu.VMEM((nchunk, N_PAIRS, LANE, CHUNK), BF16),
            pltpu.VMEM((nchunk, N_PAIRS, 8, LANE), F32),
        ],
        compiler_params=_cparams(("arbitrary", "arbitrary")),
        name="gated_delta",
    )(qn, kn, vv, gates, z, qm, km, vm, gm, gn2)


FF_SPLIT = 4


def _ffn_kernel(x_ref, mla_ref, gdn_ref, lng_ref, lnb_ref, wo_ref, l1g_ref, l1b_ref, w1_ref, w2_ref,
                l2g_ref, l2b_ref, o_ref):
    tm = x_ref.shape[0]
    nsub = SUB_TILES if tm % (SUB_TILES * CHUNK) == 0 else 1
    ts = tm // nsub
    subs = [pl.ds(i * ts, ts) for i in range(nsub)]
    mix = [_dot(mla_ref[r, :], wo_ref[0:GDN_W, :]) + _dot(gdn_ref[r, :], wo_ref[GDN_W:2 * GDN_W, :]) for r in subs]
    h1 = [_layer_norm(DN_ALPHA * _layer_norm(x_ref[r, :], lng_ref[...], lnb_ref[...]) + mix[i], l1g_ref[...],
                      l1b_ref[...]) for i, r in enumerate(subs)]
    hb = [v.astype(BF16) for v in h1]
    ffw = D_FF // FF_SPLIT
    acc = [None] * nsub
    for c in range(FF_SPLIT):
        t = [jnp.maximum(_dot(hb[i], w1_ref[:, c * ffw:(c + 1) * ffw]), 0.0) for i in range(nsub)]
        for i in range(nsub):
            part = _dot((t[i] * t[i]).astype(BF16), w2_ref[c * ffw:(c + 1) * ffw, :])
            acc[i] = part if acc[i] is None else acc[i] + part
    for i, r in enumerate(subs):
        o_ref[r, :] = _layer_norm(DN_ALPHA * h1[i] + acc[i], l2g_ref[...], l2b_ref[...])


def _ffn(x, mla, gdn, prm, *, tm):
    rows = x.shape[0]
    const = lambda i: (0, 0)
    single = dict(pipeline_mode=pl.Buffered(1))
    return pl.pallas_call(
        _ffn_kernel,
        out_shape=jax.ShapeDtypeStruct((rows, D_MODEL), F32),
        grid=(rows // tm,),
        in_specs=[
            pl.BlockSpec((tm, D_MODEL), lambda i: (i, 0)),
            pl.BlockSpec((tm, GDN_W), lambda i: (i, 0)),
            pl.BlockSpec((tm, GDN_W), lambda i: (i, 0)),
            pl.BlockSpec((1, D_MODEL), const),
            pl.BlockSpec((1, D_MODEL), const),
            pl.BlockSpec((D_MODEL, D_MODEL), const, **single),
            pl.BlockSpec((1, D_MODEL), const),
            pl.BlockSpec((1, D_MODEL), const),
            pl.BlockSpec((D_MODEL, D_FF), const, **single),
            pl.BlockSpec((D_FF, D_MODEL), const, **single),
            pl.BlockSpec((1, D_MODEL), const),
            pl.BlockSpec((1, D_MODEL), const),
        ],
        out_specs=pl.BlockSpec((tm, D_MODEL), lambda i: (i, 0)),
        compiler_params=_cparams(("arbitrary",)),
        name="outproj_ffn",
    )(x, mla, gdn, prm["ln_in_g"], prm["ln_in_b"], prm["w_out"], prm["ln1_g"], prm["ln1_b"], prm["w_ff1"],
      prm["w_ff2"], prm["ln2_g"], prm["ln2_b"])


def _pack_params(ln_in_g, ln_in_b, w_in, g_cq, g_ckv, w_uq, w_uk, w_uv, conv_w, a_log_f, a_log_b, dt_bias_f,
                 dt_bias_b, gdn_norm_g, w_out, ln1_g, ln1_b, w_ff1, w_ff2, ln2_g, ln2_b):
    l = 0
    w = w_in[l]
    o_kr = Q_LORA + KV_LORA
    o_qkv = o_kr + QK_ROPE
    o_z = o_qkv + QKV_W
    o_g = o_z + GDN_W
    zeros = lambda n: jnp.zeros((D_MODEL, n), F32)
    small = jnp.concatenate([w[:, o_g:o_g + 4 * HEADS], zeros(32), w[:, o_kr:o_kr + QK_ROPE], zeros(32)], axis=1)
    w_in_p = jnp.concatenate([w[:, :o_kr], small, w[:, o_qkv:o_z], w[:, o_z:o_g]], axis=1).astype(BF16)

    uq = w_uq[l].reshape(Q_LORA, HEADS, QK_NOPE + QK_ROPE)
    uq = jnp.pad(uq, ((0, 0), (0, 0), (0, LANE - QK_NOPE - QK_ROPE))).reshape(Q_LORA, HEADS * LANE)
    uk = w_uk[l].reshape(KV_LORA, HEADS, QK_NOPE)
    uk = jnp.pad(uk, ((0, 0), (0, 0), (0, LANE - QK_NOPE))).reshape(KV_LORA, HEADS * LANE)
    w_kv = jnp.concatenate([uk, w_uv[l]], axis=1)

    pad_to_lane = lambda v: jnp.pad(v, (0, LANE - v.shape[0])).reshape(1, LANE)
    row = lambda v: v.reshape(1, -1).astype(F32)
    return dict(
        ln_in_g=row(ln_in_g), ln_in_b=row(ln_in_b), w_in=w_in_p, g_cq=row(g_cq[l]), g_ckv=row(g_ckv[l]),
        w_uq=uq.astype(BF16), w_kv=w_kv.astype(BF16), conv_w=conv_w[l].astype(F32),
        alog=pad_to_lane(jnp.concatenate([a_log_f[l], a_log_b[l]]).astype(F32)),
        dtb=pad_to_lane(jnp.concatenate([dt_bias_f[l], dt_bias_b[l]]).astype(F32)),
        gn2=jnp.concatenate([gdn_norm_g[l], gdn_norm_g[l]]).reshape(1, LANE).astype(F32),
        w_out=w_out[l].astype(BF16), ln1_g=row(ln1_g[l]), ln1_b=row(ln1_b[l]), w_ff1=w_ff1[l].astype(BF16),
        w_ff2=w_ff2[l].astype(BF16), ln2_g=row(ln2_g[l]), ln2_b=row(ln2_b[l]),
    )


def _rope_tables(pos0, n):
    inv = ROPE_THETA ** (-jnp.arange(0, QK_ROPE, 2, dtype=F32) / QK_ROPE)
    ang = (pos0 + jnp.arange(n, dtype=F32))[:, None] * inv[None, :]
    cos, sin = jnp.cos(ang), jnp.sin(ang)
    half = QK_ROPE // 2
    z = lambda w: jnp.zeros((n, w), F32)
    c_rope = jnp.concatenate([z(QK_NOPE), cos, cos, z(LANE - QK_NOPE - QK_ROPE)], axis=1)
    s1 = jnp.concatenate([z(QK_NOPE), -sin, z(half), z(LANE - QK_NOPE - QK_ROPE)], axis=1)
    s2 = jnp.concatenate([z(QK_NOPE), z(half), sin, z(LANE - QK_NOPE - QK_ROPE)], axis=1)
    c_q = jnp.concatenate([jnp.ones((n, QK_NOPE), F32), cos, cos, z(LANE - QK_NOPE - QK_ROPE)], axis=1)
    scale = (QK_NOPE + QK_ROPE) ** -0.5 * LOG2_E
    return jnp.stack([c_q * scale, s1 * scale, s2 * scale, c_rope, s1, s2])


def _choose_tile(n, target):
    t = min(n, target)
    while n % t:
        t //= 2
    return t


def _encode_group(x, meta, prm):
    batch, seq, _ = x.shape
    rows = batch * seq
    xf = x.reshape(rows, D_MODEL)
    tm = _choose_tile(seq, ROW_TILE)
    q, k, vt, qkv_pre, z, gates = _inproj(xf, _rope_tables(float(N_META), seq), prm, tm=tm, seq_blocks=seq // tm)

    meta_qkv = meta["qkv_pre"]
    blk = _choose_tile(seq, GDN_BLOCK)
    nblk = seq // blk
    r8 = blk // 8
    zeros8 = jnp.zeros((8, QKV_W), F32)
    qn, kn, vv = _conv(
        qkv_pre, qkv_pre, qkv_pre, meta_qkv, zeros8, prm["conv_w"], batch=batch, blk=blk, nblk=nblk,
        main_map=lambda b, i: (b * nblk + i, 0),
        prev_map=lambda b, i: (jnp.maximum((b * nblk + i) * r8 - 1, 0), 0),
        next_map=lambda b, i: (jnp.minimum((b * nblk + i + 1) * r8, rows // 8 - 1), 0),
        pfirst_map=lambda b, i: (N_META // 8 - 1, 0),
        nlast_map=lambda b, i: (0, 0))
    qm, km, vm = _conv(
        meta_qkv, meta_qkv, meta_qkv, zeros8, qkv_pre, prm["conv_w"], batch=batch, blk=N_META, nblk=1,
        main_map=lambda b, i: (0, 0), prev_map=lambda b, i: (0, 0), next_map=lambda b, i: (0, 0),
        pfirst_map=lambda b, i: (0, 0), nlast_map=lambda b, i: (b * (seq // 8), 0))
    front = CHUNK - N_META
    pad_meta = lambda a: jnp.pad(a.reshape(batch, N_META, GDN_W), ((0, 0), (front, 0), (0, 0))).reshape(
        batch * CHUNK, GDN_W)
    gm = jnp.pad(meta["gates"][:, :LANE], ((front, 0), (0, 0)))
    gdn_out = _gdn(qn, kn, vv, gates, z, pad_meta(qm), pad_meta(km), pad_meta(vm), gm, prm["gn2"], batch=batch,
                   seq=seq, blk=blk)

    tq = _choose_tile(seq, max(Q_TILE, SCORE_ELEMS // seq))
    mla_out = _attention(q, k, vt, meta["k"], meta["vt"], batch=batch, seq=seq, tq=tq)
    out = _ffn(xf, mla_out, gdn_out, prm, tm=tm)
    return out.reshape(batch, seq, D_MODEL)


def kernel(x_prompt, x_sample, meta_tokens, ln_in_g, ln_in_b, w_in, g_cq, g_ckv, w_uq, w_uk, w_uv, conv_w, a_log_f,
           a_log_b, dt_bias_f, dt_bias_b, gdn_norm_g, w_out, ln1_g, ln1_b, w_ff1, w_ff2, ln2_g, ln2_b):
    prm = _pack_params(ln_in_g, ln_in_b, w_in, g_cq, g_ckv, w_uq, w_uk, w_uv, conv_w, a_log_f, a_log_b, dt_bias_f,
                       dt_bias_b, gdn_norm_g, w_out, ln1_g, ln1_b, w_ff1, w_ff2, ln2_g, ln2_b)
    _, mk, mv, mqkv, _, mgates = _inproj(meta_tokens.astype(F32), _rope_tables(0.0, N_META), prm, tm=N_META,
                                         seq_blocks=1)
    meta = dict(k=mk, vt=mv, qkv_pre=mqkv, gates=mgates)
    return (_encode_group(x_prompt, meta, prm), _encode_group(x_sample, meta, prm))
```

```python
import functools

import jax
import jax.numpy as jnp
from jax import lax
from jax.experimental import pallas as pl
from jax.experimental.pallas import tpu as pltpu

F32 = jnp.float32
BF16 = jnp.bfloat16

D_MODEL = 1024
N_META = 16
HEADS = 8
QK_NOPE = 64
QK_ROPE = 32
V_HEAD = 64
Q_LORA = 384
KV_LORA = 256
ROPE_THETA = 10000.0
GDN_DK = 64
GDN_DV = 64
CONV_K = 4
GDN_W = HEADS * GDN_DV
QKV_W = 3 * GDN_W
D_FF = 4 * D_MODEL
DEPTH = 1
DN_ALPHA = (2 * DEPTH) ** 0.25
LN_EPS = 1e-5
RMS_EPS = 1e-6
L2_EPS = 1e-6
LOG2_E = 1.4426950408889634

LANE = 128
CHUNK = 128
N_PAIRS = HEADS // 2
COL_CQ = 0
COL_CKV = COL_CQ + Q_LORA
COL_SMALL = COL_CKV + KV_LORA
COL_QKV = COL_SMALL + LANE
COL_Z = COL_QKV + QKV_W
N_IN_PACKED = COL_Z + GDN_W
VMEM_LIMIT = 56 * 1024 * 1024
ROW_TILE = 1024
SUB_TILES = 2
GDN_BLOCK = 1024
Q_TILE = 256
SCORE_ELEMS = 1 << 20
KEY_TILE = 1024
KEY_BLOCK_BYTES = 4 << 20


def _cparams(sem):
    return pltpu.CompilerParams(dimension_semantics=sem, vmem_limit_bytes=VMEM_LIMIT)


def _layer_norm(x, g, b):
    mu = jnp.mean(x, axis=-1, keepdims=True)
    xc = x - mu
    var = jnp.mean(xc * xc, axis=-1, keepdims=True)
    return xc * lax.rsqrt(var + LN_EPS) * g + b


def _rms_norm(x, g):
    return x * lax.rsqrt(jnp.mean(x * x, axis=-1, keepdims=True) + RMS_EPS) * g


def _sigmoid(x):
    return 1.0 / (1.0 + jnp.exp2(x * (-LOG2_E)))


def _softplus(x):
    return jnp.maximum(x, 0.0) + jnp.log1p(jnp.exp(-jnp.abs(x)))


def _dot(a, b):
    return jnp.dot(a, b, preferred_element_type=F32)


def _inproj_kernel(x_ref, lng_ref, lnb_ref, win_ref, gcq_ref, gckv_ref, wuq_ref, wkv_ref, tab_ref,
                   alog_ref, dtb_ref, q_ref, k_ref, vt_ref, qkv_ref, z_ref, gate_ref, *, scan_len):
    tm = x_ref.shape[0]
    h = _layer_norm(x_ref[...], lng_ref[...], lnb_ref[...])
    proj = _dot(h.astype(BF16), win_ref[...])

    qkv_ref[...] = proj[:, COL_QKV:COL_QKV + QKV_W]
    z_ref[...] = proj[:, COL_Z:COL_Z + GDN_W]

    cqn = _rms_norm(proj[:, COL_CQ:COL_CQ + Q_LORA], gcq_ref[...])
    qfull = _dot(cqn.astype(BF16), wuq_ref[...])
    cq_t, s1q_t, s2q_t = tab_ref[0], tab_ref[1], tab_ref[2]
    for hd in range(HEADS):
        qh = qfull[:, hd * LANE:(hd + 1) * LANE]
        qr = qh * cq_t + pltpu.roll(qh, LANE - 16, 1) * s1q_t + pltpu.roll(qh, 16, 1) * s2q_t
        q_ref[:, hd * LANE:(hd + 1) * LANE] = qr.astype(BF16)

    small = proj[:, COL_SMALL:COL_SMALL + LANE]
    ck_t, s1k_t, s2k_t = tab_ref[3], tab_ref[4], tab_ref[5]
    krope = small * ck_t + pltpu.roll(small, LANE - 16, 1) * s1k_t + pltpu.roll(small, 16, 1) * s2k_t
    ckvn = _rms_norm(proj[:, COL_CKV:COL_CKV + KV_LORA], gckv_ref[...])
    kv = _dot(ckvn.astype(BF16), wkv_ref[...])
    for hd in range(HEADS):
        k_ref[:, hd * LANE:(hd + 1) * LANE] = (kv[:, hd * LANE:(hd + 1) * LANE] + krope).astype(BF16)
    vt_ref[...] = kv[:, HEADS * LANE:].T.astype(BF16)

    lane = lax.broadcasted_iota(jnp.int32, (tm, LANE), 1)
    row = lax.broadcasted_iota(jnp.int32, (tm, LANE), 0)
    gval = jnp.where(lane < 16, -jnp.exp(alog_ref[...]) * _softplus(small + dtb_ref[...]), _sigmoid(small))
    pos = row % scan_len
    pre = gval
    suf = gval
    sh = 1
    while sh < scan_len:
        pre = pre + jnp.where(pos >= sh, pltpu.roll(pre, sh, 0), 0.0)
        suf = suf + jnp.where(pos < scan_len - sh, pltpu.roll(suf, tm - sh, 0), 0.0)
        sh *= 2
    gate_ref[:, 0:LANE] = jnp.where(lane < 8, pre, gval)
    gate_ref[:, LANE:2 * LANE] = pltpu.roll(jnp.where((lane >= 8) & (lane < 16), suf, gval), LANE - 8, 1)


def _inproj(x, tabs, prm, *, tm, seq_blocks):
    rows = x.shape[0]
    n = rows // tm
    const = lambda i: (0, 0)
    kernel = functools.partial(_inproj_kernel, scan_len=min(tm, CHUNK))
    return pl.pallas_call(
        kernel,
        out_shape=(
            jax.ShapeDtypeStruct((rows, HEADS * LANE), BF16),
            jax.ShapeDtypeStruct((rows, HEADS * LANE), BF16),
            jax.ShapeDtypeStruct((HEADS * V_HEAD, rows), BF16),
            jax.ShapeDtypeStruct((rows, QKV_W), F32),
            jax.ShapeDtypeStruct((rows, GDN_W), F32),
            jax.ShapeDtypeStruct((rows, 2 * LANE), F32),
        ),
        grid=(n,),
        in_specs=[
            pl.BlockSpec((tm, D_MODEL), lambda i: (i, 0)),
            pl.BlockSpec((1, D_MODEL), const),
            pl.BlockSpec((1, D_MODEL), const),
            pl.BlockSpec((D_MODEL, N_IN_PACKED), const),
            pl.BlockSpec((1, Q_LORA), const),
            pl.BlockSpec((1, KV_LORA), const),
            pl.BlockSpec((Q_LORA, HEADS * LANE), const),
            pl.BlockSpec((KV_LORA, HEADS * LANE + HEADS * V_HEAD), const),
            pl.BlockSpec((6, tm, LANE), lambda i: (0, i % seq_blocks, 0)),
            pl.BlockSpec((1, LANE), const),
            pl.BlockSpec((1, LANE), const),
        ],
        out_specs=(
            pl.BlockSpec((tm, HEADS * LANE), lambda i: (i, 0)),
            pl.BlockSpec((tm, HEADS * LANE), lambda i: (i, 0)),
            pl.BlockSpec((HEADS * V_HEAD, tm), lambda i: (0, i)),
            pl.BlockSpec((tm, QKV_W), lambda i: (i, 0)),
            pl.BlockSpec((tm, GDN_W), lambda i: (i, 0)),
            pl.BlockSpec((tm, 2 * LANE), lambda i: (i, 0)),
        ),
        compiler_params=_cparams(("arbitrary",)),
        name="inproj",
    )(x, prm["ln_in_g"], prm["ln_in_b"], prm["w_in"], prm["g_cq"], prm["g_ckv"], prm["w_uq"], prm["w_kv"],
      tabs, prm["alog"], prm["dtb"])


def _pair_rsqrt_norm(x, eps, scale):
    lane = lax.broadcasted_iota(jnp.int32, x.shape, 1)
    left = lane < 64
    sq = x * x
    sa = jnp.sum(jnp.where(left, sq, 0.0), axis=-1, keepdims=True)
    sb = jnp.sum(jnp.where(left, 0.0, sq), axis=-1, keepdims=True)
    return jnp.where(left, lax.rsqrt(sa * scale + eps), lax.rsqrt(sb * scale + eps))


def _conv_kernel(x_ref, prev_ref, next_ref, pfirst_ref, nlast_ref, w_ref, q_ref, k_ref, v_ref):
    i = pl.program_id(1)
    nblk = pl.num_programs(1)
    blk = x_ref.shape[0]
    x = x_ref[...]
    prev = jnp.where(i == 0, pfirst_ref[...], prev_ref[...])
    nxt = jnp.where(i == nblk - 1, nlast_ref[...], next_ref[...])
    row = lax.broadcasted_iota(jnp.int32, x.shape, 0)
    xm1 = jnp.where(row == 0, prev[7:8, :], pltpu.roll(x, 1, 0))
    xp1 = jnp.where(row == blk - 1, nxt[0:1, :], pltpu.roll(x, blk - 1, 0))
    xp2 = jnp.where(row == blk - 2, nxt[0:1, :], jnp.where(row == blk - 1, nxt[1:2, :], pltpu.roll(x, blk - 2, 0)))
    w = w_ref[...]
    y = xm1 * w[0:1, :] + x * w[1:2, :] + xp1 * w[2:3, :] + xp2 * w[3:4, :]
    y = y * _sigmoid(y)
    for p in range(N_PAIRS):
        qp = y[:, p * LANE:(p + 1) * LANE]
        kp = y[:, GDN_W + p * LANE:GDN_W + (p + 1) * LANE]
        q_ref[:, p * LANE:(p + 1) * LANE] = qp * _pair_rsqrt_norm(qp, L2_EPS, 1.0) * (GDN_DK ** -0.5)
        k_ref[:, p * LANE:(p + 1) * LANE] = kp * _pair_rsqrt_norm(kp, L2_EPS, 1.0)
    v_ref[...] = y[:, 2 * GDN_W:]


def _conv(x, prev_arr, next_arr, pfirst, nlast, conv_w, *, batch, blk, nblk, main_map, prev_map, next_map,
          pfirst_map, nlast_map):
    rows = batch * nblk * blk
    return pl.pallas_call(
        _conv_kernel,
        out_shape=tuple(jax.ShapeDtypeStruct((rows, GDN_W), F32) for _ in range(3)),
        grid=(batch, nblk),
        in_specs=[
            pl.BlockSpec((blk, QKV_W), main_map),
            pl.BlockSpec((8, QKV_W), prev_map),
            pl.BlockSpec((8, QKV_W), next_map),
            pl.BlockSpec((8, QKV_W), pfirst_map),
            pl.BlockSpec((8, QKV_W), nlast_map),
            pl.BlockSpec((CONV_K, QKV_W), lambda b, i: (0, 0)),
        ],
        out_specs=tuple(pl.BlockSpec((blk, GDN_W), lambda b, i: (b * nblk + i, 0)) for _ in range(3)),
        compiler_params=_cparams(("arbitrary", "arbitrary")),
        name="conv_silu_l2",
    )(x, prev_arr, next_arr, pfirst, nlast, conv_w)


def _attn_kernel(q_ref, k_ref, vt_ref, km_ref, vmt_ref, o_ref):
    dn = (((1,), (1,)), ((), ()))
    n_heads = k_ref.shape[1] // LANE
    heads = range(n_heads)
    seq = k_ref.shape[0]
    tk = min(seq, KEY_TILE)
    qs = [q_ref[:, hh * LANE:(hh + 1) * LANE] for hh in heads]

    def scores(hh, j):
        return lax.dot_general(k_ref[j * tk:(j + 1) * tk, hh * LANE:(hh + 1) * LANE], qs[hh], dn,
                               preferred_element_type=F32)

    nk = seq // tk
    m_run, l_run, acc = [None] * n_heads, [None] * n_heads, [None] * n_heads

    def update(hh, s, v_t):
        m_tile = jnp.max(s, axis=0, keepdims=True)
        m_new = m_tile if m_run[hh] is None else jnp.maximum(m_run[hh], m_tile)
        p = jnp.exp2(s - m_new)
        l_tile = jnp.sum(p, axis=0, keepdims=True)
        pv = _dot(v_t, p.astype(BF16))
        if m_run[hh] is None:
            l_run[hh], acc[hh] = l_tile, pv
        else:
            alpha = jnp.exp2(m_run[hh] - m_new)
            l_run[hh] = l_run[hh] * alpha + l_tile
            acc[hh] = acc[hh] * alpha + pv
        m_run[hh] = m_new

    st_next = [scores(hh, 0) for hh in heads]
    for j in range(nk):
        st_cur = st_next
        if j + 1 < nk:
            st_next = [scores(hh, j + 1) for hh in heads]
        else:
            st_next = [lax.dot_general(km_ref[:, hh * LANE:(hh + 1) * LANE], qs[hh], dn,
                                       preferred_element_type=F32) for hh in heads]
        for hh in heads:
            update(hh, st_cur[hh], vt_ref[hh * V_HEAD:(hh + 1) * V_HEAD, j * tk:(j + 1) * tk])
    for hh in heads:
        update(hh, st_next[hh], vmt_ref[hh * V_HEAD:(hh + 1) * V_HEAD, :])
    outs = [acc[hh] / l_run[hh] for hh in heads]
    for pr in range(n_heads // 2):
        o_ref[:, pr * LANE:(pr + 1) * LANE] = jnp.concatenate(outs[2 * pr:2 * pr + 2], axis=0).T.astype(BF16)


def _attention(q, k, vt, km, vmt, *, batch, seq, tq):
    nq = seq // tq
    rows = batch * seq
    nh = 2
    while nh < HEADS and seq * 2 * nh * LANE * 2 <= KEY_BLOCK_BYTES:
        nh *= 2
    return pl.pallas_call(
        _attn_kernel,
        out_shape=jax.ShapeDtypeStruct((rows, HEADS * V_HEAD), BF16),
        grid=(batch, HEADS // nh, nq),
        in_specs=[
            pl.BlockSpec((tq, nh * LANE), lambda b, p, i: (b * nq + i, p)),
            pl.BlockSpec((seq, nh * LANE), lambda b, p, i: (b, p)),
            pl.BlockSpec((nh * V_HEAD, seq), lambda b, p, i: (p, b)),
            pl.BlockSpec((N_META, nh * LANE), lambda b, p, i: (0, p)),
            pl.BlockSpec((nh * V_HEAD, N_META), lambda b, p, i: (p, 0)),
        ],
        out_specs=pl.BlockSpec((tq, nh * V_HEAD), lambda b, p, i: (b * nq + i, p)),
        compiler_params=_cparams(("arbitrary", "arbitrary", "arbitrary")),
        name="mla_attention",
    )(q, k, vt, km, vmt)


def _blockdiag_rows(x):
    lane = lax.broadcasted_iota(jnp.int32, x.shape, 1)
    zero = jnp.zeros_like(x)
    return jnp.concatenate([jnp.where(lane < 64, x, zero), jnp.where(lane < 64, zero, x)], axis=0)


def _gdn_kernel(q_ref, k_ref, v_ref, g_ref, z_ref, qm_ref, km_ref, vm_ref, gm_ref, gn_ref, o_ref,
                s_scr, of_scr, ob_scr, u_scr, wq_scr, attn_scr, kd_scr, egl_scr, *, nblk, nchunk, unroll):
    t = pl.program_id(1)
    is_meta = t == 0
    is_bwd = t > nblk
    blk_idx = jnp.where(is_bwd, 2 * nblk - t, jnp.maximum(t - 1, 0))
    blk = nchunk * CHUNK

    @pl.when(jnp.logical_or(t == 0, t == nblk + 1))
    def _():
        s_scr[...] = jnp.zeros_like(s_scr)

    row_i = lax.broadcasted_iota(jnp.int32, (CHUNK, CHUNK), 0)
    col_i = lax.broadcasted_iota(jnp.int32, (CHUNK, CHUNK), 1)
    sgn = jnp.where(is_bwd, -1, 1)
    d_rc = (col_i - row_i) * sgn
    incl = d_rc <= 0
    strict = d_rc < 0
    incl2 = jnp.concatenate([incl, incl], axis=1)
    strict2 = jnp.concatenate([strict, strict], axis=1)
    left = col_i < 64
    top = row_i < 64
    eye = (row_i == col_i).astype(F32)
    eye2 = jnp.concatenate([eye, eye], axis=1)
    bd_mask = top == left
    same_blk = [jnp.concatenate([(row_i >> k) == (col_i >> k)] * 2, axis=1) for k in range(1, 8)]
    gn = gn_ref[...]

    def colb(arr, lane_idx):
        return jnp.broadcast_to(arr[:, lane_idx:lane_idx + 1], (CHUNK, CHUNK))

    def rowb(arr, row_idx):
        return jnp.broadcast_to(arr[row_idx:row_idx + 1, :], (CHUNK, CHUNK))

    def bd2(x):
        zero = jnp.zeros((CHUNK, CHUNK), x.dtype)
        return jnp.concatenate([jnp.concatenate([x[:, :CHUNK], zero], axis=1),
                                jnp.concatenate([zero, x[:, CHUNK:]], axis=1)], axis=0)

    def load_chunk(c):
        rows = pl.ds(pl.multiple_of(c * CHUNK, CHUNK), CHUNK)
        is_meta_chunk = c == 0
        q_c = jnp.where(is_meta, jnp.where(is_meta_chunk, qm_ref[...], 0.0), q_ref[rows, :])
        k_c = jnp.where(is_meta, jnp.where(is_meta_chunk, km_ref[...], 0.0), k_ref[rows, :])
        v_c = jnp.where(is_meta, jnp.where(is_meta_chunk, vm_ref[...], 0.0), v_ref[rows, :])
        g_c = jnp.where(is_meta, jnp.where(is_meta_chunk, gm_ref[...], 0.0), g_ref[rows, :])
        g_t = g_c.T
        k_t = k_c.T
        gl_row = jnp.where(is_bwd, g_c[0:1, :], g_c[CHUNK - 1:CHUNK, :])
        gl_col = jnp.where(is_bwd, g_t[:, 0:1], g_t[:, CHUNK - 1:CHUNK])
        egc = jnp.exp(g_c)
        ekd_t = jnp.exp(gl_col - g_t)
        egl = jnp.exp(gl_row)
        return dict(c=c, q=q_c, k=k_c, v=v_c, g=g_c, g_t=g_t, k_t=k_t, egc=egc, ekd_t=ekd_t, egl=egl)

    def prep_stages(gi):
        chunks = [load_chunk(gi * unroll + j) for j in range(unroll)]
        chains = [dict(ch=ch, p=p) for ch in chunks for p in range(N_PAIRS)]

        for cn in chains:
            ch, p = cn["ch"], cn["p"]
            ha, hb = 2 * p, 2 * p + 1
            g_c, g_t = ch["g"], ch["g_t"]
            q2 = ch["q"][:, p * LANE:(p + 1) * LANE]
            k2 = ch["k"][:, p * LANE:(p + 1) * LANE]
            v2 = ch["v"][:, p * LANE:(p + 1) * LANE]
            kt2 = ch["k_t"][p * LANE:(p + 1) * LANE, :]
            gdiff2 = jnp.concatenate([colb(g_c, ha) - rowb(g_t, ha), colb(g_c, hb) - rowb(g_t, hb)], axis=1)
            dec2 = jnp.exp(jnp.where(incl2, gdiff2, -jnp.inf))
            beta_a = colb(g_c, 16 + ha)
            beta_b = colb(g_c, 16 + hb)
            beta_p = jnp.where(left, beta_a, beta_b)
            egc_p = jnp.where(left, colb(ch["egc"], ha), colb(ch["egc"], hb))
            zero_t = jnp.zeros_like(kt2)
            w_qk = jnp.concatenate([jnp.where(top, kt2, zero_t), jnp.where(top, zero_t, kt2)], axis=1)
            qkkk = _dot(jnp.concatenate([q2, k2], axis=0).astype(BF16), w_qk.astype(BF16))
            attn_scr[ch["c"], p] = (qkkk[:CHUNK] * dec2).astype(BF16)
            m2 = jnp.where(strict2, jnp.concatenate([beta_a, beta_b], axis=1) * qkkk[CHUNK:] * dec2, 0.0)
            cn["m2"] = m2
            cn["tinv"] = eye2 - jnp.where(same_blk[0], m2, 0.0)
            vb2 = v2 * beta_p
            kbe2 = k2 * beta_p * egc_p
            cn["w_rhs"] = jnp.concatenate([_blockdiag_rows(vb2), _blockdiag_rows(kbe2)], axis=1).astype(BF16)
            cn["qg"] = (q2 * egc_p).astype(BF16)
            ekd_rows = jnp.where(top, rowb(ch["ekd_t"], ha), rowb(ch["ekd_t"], hb))
            kd_scr[ch["c"], p] = (kt2 * ekd_rows).astype(BF16)
            egl_scr[ch["c"], p] = jnp.where(left[0:8, :], jnp.broadcast_to(ch["egl"][:, ha:ha + 1], (8, LANE)),
                                            jnp.broadcast_to(ch["egl"][:, hb:hb + 1], (8, LANE)))
        yield

        for k in range(1, len(same_blk)):
            for cn in chains:
                join = jnp.where(jnp.logical_and(same_blk[k], jnp.logical_not(same_blk[k - 1])), cn["m2"], 0.0)
                cn["tb"] = cn["tinv"].astype(BF16)
                cn["tc"] = _dot(cn["tb"], bd2(join.astype(BF16)))
            yield
            for cn in chains:
                cn["tinv"] = cn["tinv"] - _dot(cn["tc"].astype(BF16), bd2(cn["tb"]))
            yield

        for cn in chains:
            uw = _dot(cn["tinv"].astype(BF16), cn["w_rhs"])
            c, p = cn["ch"]["c"], cn["p"]
            u_scr[c, p] = uw[:, :LANE]
            wq_scr[c, p] = jnp.concatenate([uw[:, LANE:].astype(BF16), cn["qg"]], axis=0)

    def state_stages(gi):
        for j in range(unroll):
            c = gi * unroll + jnp.where(is_bwd, unroll - 1 - j, j)
            rows = pl.ds(pl.multiple_of(c * CHUNK, CHUNK), CHUNK)
            states = [s_scr[p] for p in range(N_PAIRS)]
            ws_qs = [_dot(wq_scr[c, p], states[p].astype(BF16)) for p in range(N_PAIRS)]
            yield
            vnew = [(u_scr[c, p] - ws_qs[p][:CHUNK]).astype(BF16) for p in range(N_PAIRS)]
            for p in range(N_PAIRS):
                ob_scr[rows, p * LANE:(p + 1) * LANE] = (ws_qs[p][CHUNK:]
                                                         + _dot(attn_scr[c, p], _blockdiag_rows(vnew[p])))
            for p in range(N_PAIRS):
                s_scr[p] = states[p] * egl_scr[c, p][0:1, :] + jnp.where(bd_mask, _dot(kd_scr[c, p], vnew[p]), 0.0)
            yield

    def run_interleaved(*gens):
        live = list(gens)
        while live:
            for g in list(live):
                try:
                    next(g)
                except StopIteration:
                    live.remove(g)

    ngroups = nchunk // unroll
    n_order = jnp.where(is_meta, 1, ngroups)

    def group_at(i):
        return jnp.where(is_bwd, ngroups - 1 - i, i)

    def skewed(i, carry):
        run_interleaved(prep_stages(group_at(i)), state_stages(group_at(i - 1)))
        return carry

    run_interleaved(prep_stages(group_at(0)))
    lax.fori_loop(1, n_order, skewed, 0)
    run_interleaved(state_stages(group_at(n_order - 1)))

    blk_rows = pl.ds(pl.multiple_of(blk_idx * blk, CHUNK), blk)

    @pl.when(jnp.logical_and(jnp.logical_not(is_bwd), jnp.logical_not(is_meta)))
    def _():
        of_scr[blk_rows, :] = ob_scr[...]

    @pl.when(is_bwd)
    def _():
        for p in range(N_PAIRS):
            o_tot = of_scr[blk_rows, p * LANE:(p + 1) * LANE] + ob_scr[:, p * LANE:(p + 1) * LANE]
            inv = _pair_rsqrt_norm(o_tot, RMS_EPS, 1.0 / GDN_DV)
            z2 = z_ref[:, p * LANE:(p + 1) * LANE]
            o_ref[:, p * LANE:(p + 1) * LANE] = (o_tot * inv * gn * (z2 * _sigmoid(z2))).astype(BF16)


def _gdn(qn, kn, vv, gates, z, qm, km, vm, gm, gn2, *, batch, seq, blk):
    nblk = seq // blk
    nchunk = blk // CHUNK
    rows = batch * seq

    def blk_of(t):
        return jnp.where(t > nblk, 2 * nblk - t, jnp.maximum(t - 1, 0))

    main_map = lambda b, t: (b * nblk + blk_of(t), 0)
    gate_map = lambda b, t: (b * nblk + blk_of(t), jnp.where(t > nblk, 1, 0))
    late_map = lambda b, t: (b * nblk + jnp.where(t > nblk, 2 * nblk - t, nblk - 1), 0)
    unroll = 2 if nchunk % 2 == 0 else 1
    kernel = functools.partial(_gdn_kernel, nblk=nblk, nchunk=nchunk, unroll=unroll)
    return pl.pallas_call(
        kernel,
        out_shape=jax.ShapeDtypeStruct((rows, GDN_W), BF16),
        grid=(batch, 2 * nblk + 1),
        in_specs=[
            pl.BlockSpec((blk, GDN_W), main_map),
            pl.BlockSpec((blk, GDN_W), main_map),
            pl.BlockSpec((blk, GDN_W), main_map),
            pl.BlockSpec((blk, LANE), gate_map),
            pl.BlockSpec((blk, GDN_W), late_map),
            pl.BlockSpec((CHUNK, GDN_W), lambda b, t: (b, 0)),
            pl.BlockSpec((CHUNK, GDN_W), lambda b, t: (b, 0)),
            pl.BlockSpec((CHUNK, GDN_W), lambda b, t: (b, 0)),
            pl.BlockSpec((CHUNK, LANE), lambda b, t: (0, 0)),
            pl.BlockSpec((1, LANE), lambda b, t: (0, 0)),
        ],
        out_specs=pl.BlockSpec((blk, GDN_W), late_map),
        scratch_shapes=[
            pltpu.VMEM((N_PAIRS, LANE, LANE), F32),
            pltpu.VMEM((seq, GDN_W), F32),
            pltpu.VMEM((blk, GDN_W), F32),
            pltpu.VMEM((nchunk, N_PAIRS, CHUNK, LANE), F32),
            pltpu.VMEM((nchunk, N_PAIRS, 2 * CHUNK, LANE), BF16),
            pltpu.VMEM((nchunk, N_PAIRS, CHUNK, 2 * CHUNK), BF16),
            pltpu.VMEM((nchunk, N_PAIRS, LANE, CHUNK), BF16),
            pltpu.VMEM((nchunk, N_PAIRS, 8, LANE), F32),
        ],
        compiler_params=_cparams(("arbitrary", "arbitrary")),
        name="gated_delta",
    )(qn, kn, vv, gates, z, qm, km, vm, gm, gn2)


FF_SPLIT = 4


def _ffn_kernel(x_ref, mla_ref, gdn_ref, lng_ref, lnb_ref, wo_ref, l1g_ref, l1b_ref, w1_ref, w2_ref,
                l2g_ref, l2b_ref, o_ref):
    tm = x_ref.shape[0]
    nsub = SUB_TILES if tm % (SUB_TILES * CHUNK) == 0 else 1
    ts = tm // nsub
    subs = [pl.ds(i * ts, ts) for i in range(nsub)]
    mix = [_dot(mla_ref[r, :], wo_ref[0:GDN_W, :]) + _dot(gdn_ref[r, :], wo_ref[GDN_W:2 * GDN_W, :]) for r in subs]
    h1 = [_layer_norm(DN_ALPHA * _layer_norm(x_ref[r, :], lng_ref[...], lnb_ref[...]) + mix[i], l1g_ref[...],
                      l1b_ref[...]) for i, r in enumerate(subs)]
    hb = [v.astype(BF16) for v in h1]
    ffw = D_FF // FF_SPLIT
    acc = [None] * nsub
    for c in range(FF_SPLIT):
        t = [jnp.maximum(_dot(hb[i], w1_ref[:, c * ffw:(c + 1) * ffw]), 0.0) for i in range(nsub)]
        for i in range(nsub):
            part = _dot((t[i] * t[i]).astype(BF16), w2_ref[c * ffw:(c + 1) * ffw, :])
            acc[i] = part if acc[i] is None else acc[i] + part
    for i, r in enumerate(subs):
        o_ref[r, :] = _layer_norm(DN_ALPHA * h1[i] + acc[i], l2g_ref[...], l2b_ref[...])


def _ffn(x, mla, gdn, prm, *, tm):
    rows = x.shape[0]
    const = lambda i: (0, 0)
    single = dict(pipeline_mode=pl.Buffered(1))
    return pl.pallas_call(
        _ffn_kernel,
        out_shape=jax.ShapeDtypeStruct((rows, D_MODEL), F32),
        grid=(rows // tm,),
        in_specs=[
            pl.BlockSpec((tm, D_MODEL), lambda i: (i, 0)),
            pl.BlockSpec((tm, GDN_W), lambda i: (i, 0)),
            pl.BlockSpec((tm, GDN_W), lambda i: (i, 0)),
            pl.BlockSpec((1, D_MODEL), const),
            pl.BlockSpec((1, D_MODEL), const),
            pl.BlockSpec((D_MODEL, D_MODEL), const, **single),
            pl.BlockSpec((1, D_MODEL), const),
            pl.BlockSpec((1, D_MODEL), const),
            pl.BlockSpec((D_MODEL, D_FF), const, **single),
            pl.BlockSpec((D_FF, D_MODEL), const, **single),
            pl.BlockSpec((1, D_MODEL), const),
            pl.BlockSpec((1, D_MODEL), const),
        ],
        out_specs=pl.BlockSpec((tm, D_MODEL), lambda i: (i, 0)),
        compiler_params=_cparams(("arbitrary",)),
        name="outproj_ffn",
    )(x, mla, gdn, prm["ln_in_g"], prm["ln_in_b"], prm["w_out"], prm["ln1_g"], prm["ln1_b"], prm["w_ff1"],
      prm["w_ff2"], prm["ln2_g"], prm["ln2_b"])


def _pack_params(ln_in_g, ln_in_b, w_in, g_cq, g_ckv, w_uq, w_uk, w_uv, conv_w, a_log_f, a_log_b, dt_bias_f,
                 dt_bias_b, gdn_norm_g, w_out, ln1_g, ln1_b, w_ff1, w_ff2, ln2_g, ln2_b):
    l = 0
    w = w_in[l]
    o_kr = Q_LORA + KV_LORA
    o_qkv = o_kr + QK_ROPE
    o_z = o_qkv + QKV_W
    o_g = o_z + GDN_W
    zeros = lambda n: jnp.zeros((D_MODEL, n), F32)
    small = jnp.concatenate([w[:, o_g:o_g + 4 * HEADS], zeros(32), w[:, o_kr:o_kr + QK_ROPE], zeros(32)], axis=1)
    w_in_p = jnp.concatenate([w[:, :o_kr], small, w[:, o_qkv:o_z], w[:, o_z:o_g]], axis=1).astype(BF16)

    uq = w_uq[l].reshape(Q_LORA, HEADS, QK_NOPE + QK_ROPE)
    uq = jnp.pad(uq, ((0, 0), (0, 0), (0, LANE - QK_NOPE - QK_ROPE))).reshape(Q_LORA, HEADS * LANE)
    uk = w_uk[l].reshape(KV_LORA, HEADS, QK_NOPE)
    uk = jnp.pad(uk, ((0, 0), (0, 0), (0, LANE - QK_NOPE))).reshape(KV_LORA, HEADS * LANE)
    w_kv = jnp.concatenate([uk, w_uv[l]], axis=1)

    pad_to_lane = lambda v: jnp.pad(v, (0, LANE - v.shape[0])).reshape(1, LANE)
    row = lambda v: v.reshape(1, -1).astype(F32)
    return dict(
        ln_in_g=row(ln_in_g), ln_in_b=row(ln_in_b), w_in=w_in_p, g_cq=row(g_cq[l]), g_ckv=row(g_ckv[l]),
        w_uq=uq.astype(BF16), w_kv=w_kv.astype(BF16), conv_w=conv_w[l].astype(F32),
        alog=pad_to_lane(jnp.concatenate([a_log_f[l], a_log_b[l]]).astype(F32)),
        dtb=pad_to_lane(jnp.concatenate([dt_bias_f[l], dt_bias_b[l]]).astype(F32)),
        gn2=jnp.concatenate([gdn_norm_g[l], gdn_norm_g[l]]).reshape(1, LANE).astype(F32),
        w_out=w_out[l].astype(BF16), ln1_g=row(ln1_g[l]), ln1_b=row(ln1_b[l]), w_ff1=w_ff1[l].astype(BF16),
        w_ff2=w_ff2[l].astype(BF16), ln2_g=row(ln2_g[l]), ln2_b=row(ln2_b[l]),
    )


def _rope_tables(pos0, n):
    inv = ROPE_THETA ** (-jnp.arange(0, QK_ROPE, 2, dtype=F32) / QK_ROPE)
    ang = (pos0 + jnp.arange(n, dtype=F32))[:, None] * inv[None, :]
    cos, sin = jnp.cos(ang), jnp.sin(ang)
    half = QK_ROPE // 2
    z = lambda w: jnp.zeros((n, w), F32)
    c_rope = jnp.concatenate([z(QK_NOPE), cos, cos, z(LANE - QK_NOPE - QK_ROPE)], axis=1)
    s1 = jnp.concatenate([z(QK_NOPE), -sin, z(half), z(LANE - QK_NOPE - QK_ROPE)], axis=1)
    s2 = jnp.concatenate([z(QK_NOPE), z(half), sin, z(LANE - QK_NOPE - QK_ROPE)], axis=1)
    c_q = jnp.concatenate([jnp.ones((n, QK_NOPE), F32), cos, cos, z(LANE - QK_NOPE - QK_ROPE)], axis=1)
    scale = (QK_NOPE + QK_ROPE) ** -0.5 * LOG2_E
    return jnp.stack([c_q * scale, s1 * scale, s2 * scale, c_rope, s1, s2])


def _choose_tile(n, target):
    t = min(n, target)
    while n % t:
        t //= 2
    return t


def _encode_group(x, meta, prm):
    batch, seq, _ = x.shape
    rows = batch * seq
    xf = x.reshape(rows, D_MODEL)
    tm = _choose_tile(seq, ROW_TILE)
    q, k, vt, qkv_pre, z, gates = _inproj(xf, _rope_tables(float(N_META), seq), prm, tm=tm, seq_blocks=seq // tm)

    meta_qkv = meta["qkv_pre"]
    blk = _choose_tile(seq, GDN_BLOCK)
    nblk = seq // blk
    r8 = blk // 8
    zeros8 = jnp.zeros((8, QKV_W), F32)
    qn, kn, vv = _conv(
        qkv_pre, qkv_pre, qkv_pre, meta_qkv, zeros8, prm["conv_w"], batch=batch, blk=blk, nblk=nblk,
        main_map=lambda b, i: (b * nblk + i, 0),
        prev_map=lambda b, i: (jnp.maximum((b * nblk + i) * r8 - 1, 0), 0),
        next_map=lambda b, i: (jnp.minimum((b * nblk + i + 1) * r8, rows // 8 - 1), 0),
        pfirst_map=lambda b, i: (N_META // 8 - 1, 0),
        nlast_map=lambda b, i: (0, 0))
    qm, km, vm = _conv(
        meta_qkv, meta_qkv, meta_qkv, zeros8, qkv_pre, prm["conv_w"], batch=batch, blk=N_META, nblk=1,
        main_map=lambda b, i: (0, 0), prev_map=lambda b, i: (0, 0), next_map=lambda b, i: (0, 0),
        pfirst_map=lambda b, i: (0, 0), nlast_map=lambda b, i: (b * (seq // 8), 0))
    front = CHUNK - N_META
    pad_meta = lambda a: jnp.pad(a.reshape(batch, N_META, GDN_W), ((0, 0), (front, 0), (0, 0))).reshape(
        batch * CHUNK, GDN_W)
    gm = jnp.pad(meta["gates"][:, :LANE], ((front, 0), (0, 0)))
    gdn_out = _gdn(qn, kn, vv, gates, z, pad_meta(qm), pad_meta(km), pad_meta(vm), gm, prm["gn2"], batch=batch,
                   seq=seq, blk=blk)

    tq = _choose_tile(seq, max(Q_TILE, SCORE_ELEMS // seq))
    mla_out = _attention(q, k, vt, meta["k"], meta["vt"], batch=batch, seq=seq, tq=tq)
    out = _ffn(xf, mla_out, gdn_out, prm, tm=tm)
    return out.reshape(batch, seq, D_MODEL)


def kernel(x_prompt, x_sample, meta_tokens, ln_in_g, ln_in_b, w_in, g_cq, g_ckv, w_uq, w_uk, w_uv, conv_w, a_log_f,
           a_log_b, dt_bias_f, dt_bias_b, gdn_norm_g, w_out, ln1_g, ln1_b, w_ff1, w_ff2, ln2_g, ln2_b):
    prm = _pack_params(ln_in_g, ln_in_b, w_in, g_cq, g_ckv, w_uq, w_uk, w_uv, conv_w, a_log_f, a_log_b, dt_bias_f,
                       dt_bias_b, gdn_norm_g, w_out, ln1_g, ln1_b, w_ff1, w_ff2, ln2_g, ln2_b)
    _, mk, mv, mqkv, _, mgates = _inproj(meta_tokens.astype(F32), _rope_tables(0.0, N_META), prm, tm=N_META,
                                         seq_blocks=1)
    meta = dict(k=mk, vt=mv, qkv_pre=mqkv, gates=mgates)
    return (_encode_group(x_prompt, meta, prm), _encode_group(x_sample, meta, prm))
```
